```python
import jax, jax.numpy as jnp
from jax import lax
import numpy as np

D_MODEL = 1024
BATCH = 8
SEQ = 2048
DEPTH = 2
DEC_BATCH = 1
DEC_SEQ = 16384
PAST_LEN = 128

CHUNK = 128
D_A = D_MODEL // 2
A_HEAD_DIM = 128
A_HEADS = D_A // A_HEAD_DIM
D_B = D_MODEL // 2
B_GROUP_DIM = 128
B_GROUPS = D_B // B_GROUP_DIM
D_PROJ_AB = 2 * D_A + D_B
D_MIX_AB = D_A + D_B
D_C = D_MODEL
CONV_W = 3
D_FF = ((8 * D_MODEL // 3 + 127) // 128) * 128
N_EVEN = (DEPTH + 1) // 2
N_ODD = DEPTH // 2
EPS = 1e-6

kernel_name = "hybrid_gmlp_fnet_shortconv_encoder"


def rmsnorm(x, g):
    xf = x.astype(jnp.float32)
    y = xf * lax.rsqrt(jnp.mean(xf * xf, axis=-1, keepdims=True) + EPS)
    return (y * g.astype(jnp.float32)).astype(x.dtype)


def dwconv3(x, w, b):
    xp = jnp.pad(x, ((0, 0), (1, 1), (0, 0)))
    return xp[:, :-2] * w[0] + xp[:, 1:-1] * w[1] + xp[:, 2:] * w[2] + b


def mixer_ab(h, w_in, sgu_gain, w_s, b_s, w_out):
    Bn, S, _ = h.shape
    p = h @ w_in
    u = jax.nn.gelu(p[..., :D_A])
    v = jax.nn.gelu(p[..., D_A:2 * D_A])
    f = p[..., 2 * D_A:]
    v = rmsnorm(v.reshape(Bn, S, A_HEADS, A_HEAD_DIM), sgu_gain)
    vc = v.reshape(Bn, S // CHUNK, CHUNK, A_HEADS, A_HEAD_DIM)
    s = jnp.einsum('hpq,bnqhd->bnphd', w_s, vc) + jnp.transpose(b_s)[None, None, :, :, None]
    a = u * s.reshape(Bn, S, D_A)
    fr = f.reshape(Bn, S, B_GROUPS, B_GROUP_DIM).astype(jnp.float32)
    ff = jnp.fft.fftn(fr, axes=(1, 3), norm='ortho').real.astype(h.dtype).reshape(Bn, S, D_B)
    return jnp.concatenate([a, ff], axis=-1) @ w_out


def mixer_c(h, w_in, conv_w, conv_b, w_out):
    p = h @ w_in
    gate_b = p[..., :D_C]
    gate_c = p[..., D_C:2 * D_C]
    z = p[..., 2 * D_C:]
    y = gate_b * dwconv3(gate_c * z, conv_w, conv_b)
    return y @ w_out


def conv_ffn(h, w_up, conv_w, conv_b, w_down):
    hu = dwconv3(h @ w_up, conv_w, conv_b)
    g = hu[..., :D_FF]
    val = hu[..., D_FF:]
    return (jax.nn.silu(g) * val) @ w_down


def trunk(x, norm_mix, w_in_ab, sgu_gain, w_s, b_s, w_out_ab,
          w_in_c, conv_w_c, conv_b_c, w_out_c,
          norm_ffn, w_up, ffn_conv_w, ffn_conv_b, w_down, final_norm):
    for l in range(DEPTH):
        h = rmsnorm(x, norm_mix[l])
        if l % 2 == 0:
            i = l // 2
            x = x + mixer_ab(h, w_in_ab[i], sgu_gain[i], w_s[i], b_s[i], w_out_ab[i])
        else:
            i = l // 2
            x = x + mixer_c(h, w_in_c[i], conv_w_c[i], conv_b_c[i], w_out_c[i])
        h = rmsnorm(x, norm_ffn[l])
        x = x + conv_ffn(h, w_up[l], ffn_conv_w[l], ffn_conv_b[l], w_down[l])
    return rmsnorm(x, final_norm)


def setup_inputs(seed: int = 0) -> dict:
    key = jax.random.key(seed)
    ks = jax.random.split(key, 20)
    f32 = jnp.float32
    nrm = lambda k, shape, scale: jax.random.normal(k, shape, f32) * scale
    return {
        "x_prompt": jax.random.normal(ks[0], (BATCH, SEQ, D_MODEL), f32),
        "x_sample": jax.random.normal(ks[1], (DEC_BATCH, DEC_SEQ, D_MODEL), f32),
        "norm_mix": 1.0 + nrm(ks[2], (DEPTH, D_MODEL), 0.02),
        "w_in_ab": nrm(ks[3], (N_EVEN, D_MODEL, D_PROJ_AB), D_MODEL ** -0.5),
        "sgu_gain": 1.0 + nrm(ks[4], (N_EVEN, A_HEADS, A_HEAD_DIM), 0.02),
        "w_s": nrm(ks[5], (N_EVEN, A_HEADS, CHUNK, CHUNK), CHUNK ** -0.5),
        "b_s": 1.0 + nrm(ks[6], (N_EVEN, A_HEADS, CHUNK), 0.02),
        "w_out_ab": nrm(ks[7], (N_EVEN, D_MIX_AB, D_MODEL), D_MIX_AB ** -0.5),
        "w_in_c": nrm(ks[8], (N_ODD, D_MODEL, 3 * D_C), D_MODEL ** -0.5),
        "conv_w_c": nrm(ks[9], (N_ODD, CONV_W, D_C), CONV_W ** -0.5),
        "conv_b_c": nrm(ks[10], (N_ODD, D_C), 0.02),
        "w_out_c": nrm(ks[11], (N_ODD, D_C, D_MODEL), D_C ** -0.5),
        "norm_ffn": 1.0 + nrm(ks[12], (DEPTH, D_MODEL), 0.02),
        "w_up": nrm(ks[13], (DEPTH, D_MODEL, 2 * D_FF), D_MODEL ** -0.5),
        "ffn_conv_w": nrm(ks[14], (DEPTH, CONV_W, 2 * D_FF), CONV_W ** -0.5),
        "ffn_conv_b": nrm(ks[15], (DEPTH, 2 * D_FF), 0.02),
        "w_down": nrm(ks[16], (DEPTH, D_FF, D_MODEL), D_FF ** -0.5),
        "final_norm": 1.0 + nrm(ks[17], (D_MODEL,), 0.02),
    }


def reference(x_prompt, x_sample, norm_mix, w_in_ab, sgu_gain, w_s, b_s, w_out_ab,
              w_in_c, conv_w_c, conv_b_c, w_out_c,
              norm_ffn, w_up, ffn_conv_w, ffn_conv_b, w_down, final_norm):
    y_prompt = trunk(x_prompt, norm_mix, w_in_ab, sgu_gain, w_s, b_s, w_out_ab,
                     w_in_c, conv_w_c, conv_b_c, w_out_c,
                     norm_ffn, w_up, ffn_conv_w, ffn_conv_b, w_down, final_norm)
    y_sample = trunk(x_sample, norm_mix, w_in_ab, sgu_gain, w_s, b_s, w_out_ab,
                     w_in_c, conv_w_c, conv_b_c, w_out_c,
                     norm_ffn, w_up, ffn_conv_w, ffn_conv_b, w_down, final_norm)
    return (y_prompt, y_sample)
```

```python
import functools

import numpy as np
import jax
import jax.numpy as jnp
from jax import lax
from jax.experimental import pallas as pl
from jax.experimental.pallas import tpu as pltpu

D_MODEL = 1024
CHUNK = 128
D_A = D_MODEL // 2
A_HEAD_DIM = 128
A_HEADS = D_A // A_HEAD_DIM
D_B = D_MODEL // 2
B_GROUP_DIM = 128
B_GROUPS = D_B // B_GROUP_DIM
D_C = D_MODEL
D_FF = ((8 * D_MODEL // 3 + 127) // 128) * 128
EPS = 1e-6

F32 = jnp.float32
BF16 = jnp.bfloat16

V7X_VMEM_BYTES = 64 * 1024 * 1024
V7X_MXU_DIM = 256
BF16_SUBLANES = 16

HALO = BF16_SUBLANES
TOKEN_TILE = 512
FF_COLS = V7X_MXU_DIM
DFT_N2 = 128
DFT_COL_BLOCK_BYTES = 2 * 1024 * 1024
DFT_K1_BLOCK = 16
VMEM_LIMIT = V7X_VMEM_BYTES - 8 * 1024 * 1024


def _params(n_axes=1):
    return pltpu.CompilerParams(
        dimension_semantics=("parallel",) * n_axes, vmem_limit_bytes=VMEM_LIMIT)


def _resident(shape):
    zeros = (0,) * len(shape)
    return pl.BlockSpec(shape, lambda *_: zeros, pipeline_mode=pl.Buffered(1))


def _rmsnorm(x, g):
    return x * lax.rsqrt(jnp.mean(x * x, axis=-1, keepdims=True) + EPS) * g


def _gelu_tanh(x):
    return x * (0.5 * (1.0 + jnp.tanh(0.7978845608028654 * (x + 0.044715 * (x * x * x)))))


def _silu(x):
    return x * (1.0 / (1.0 + jnp.exp(-x)))


def _dot(a, b):
    return jnp.dot(a, b, preferred_element_type=F32)


@functools.lru_cache(maxsize=None)
def _channel_dft():
    n = B_GROUP_DIM
    k = (np.arange(n)[:, None] * np.arange(n)[None, :]) % n
    ang = 2.0 * np.pi * k / n
    m = np.concatenate([np.cos(ang), np.sin(ang)], axis=1) / np.sqrt(n)
    return m.astype(np.float32)


@functools.lru_cache(maxsize=None)
def _stage1_dft(n1):
    k = (np.arange(n1)[:, None] * np.arange(n1)[None, :]) % n1
    ang = 2.0 * np.pi * k / n1
    c, s = np.cos(ang), np.sin(ang)
    m = np.block([[c, -s], [-s, -c]]) / np.sqrt(n1)
    return m.astype(np.float32)


@functools.lru_cache(maxsize=None)
def _stage2_dft(n1, n2):
    s = n1 * n2
    k1 = np.arange(n1)[:, None, None]
    k2 = np.arange(n2)[None, :, None]
    s2 = np.arange(n2)[None, None, :]
    ang = 2.0 * np.pi * ((s2 * (k1 + n1 * k2)) % s) / s
    m = np.concatenate([np.cos(ang), np.sin(ang)], axis=2) / np.sqrt(n2)
    return m.astype(np.float32)


def _mixer_ab_front_kernel(x_ref, g_ref, win_ref, gain_ref, ws_ref, bs_ref, cs_ref,
                           a_ref, p_ref, q_ref, *, tm):
    h = _rmsnorm(x_ref[...], g_ref[...]).astype(BF16)
    proj = _dot(h, win_ref[...])
    for hd in range(A_HEADS):
        lanes = slice(hd * A_HEAD_DIM, (hd + 1) * A_HEAD_DIM)
        u = _gelu_tanh(proj[:, hd * A_HEAD_DIM:(hd + 1) * A_HEAD_DIM])
        v = _gelu_tanh(proj[:, D_A + hd * A_HEAD_DIM:D_A + (hd + 1) * A_HEAD_DIM])
        vn = _rmsnorm(v, gain_ref[:, lanes]).astype(BF16)
        for n in range(tm // CHUNK):
            rows = slice(n * CHUNK, (n + 1) * CHUNK)
            s = _dot(ws_ref[hd], vn[rows, :]) + bs_ref[:, lanes]
            a_ref[rows, lanes] = (u[rows, :] * s).astype(BF16)
    for grp in range(B_GROUPS):
        lanes = slice(grp * B_GROUP_DIM, (grp + 1) * B_GROUP_DIM)
        f = proj[:, 2 * D_A + grp * B_GROUP_DIM:2 * D_A + (grp + 1) * B_GROUP_DIM]
        pq = _dot(f.astype(BF16), cs_ref[...])
        p_ref[:, lanes] = pq[:, :B_GROUP_DIM].astype(BF16)
        q_ref[:, lanes] = pq[:, B_GROUP_DIM:].astype(BF16)


def _mixer_ab_front(x, g, w_in, gain, w_s, bs_tile, cs, *, tm):
    t = x.shape[0]
    row_f32 = pl.BlockSpec((tm, D_MODEL), lambda i: (i, 0))
    row_half = pl.BlockSpec((tm, D_A), lambda i: (i, 0))
    out = jax.ShapeDtypeStruct((t, D_A), BF16)
    return pl.pallas_call(
        functools.partial(_mixer_ab_front_kernel, tm=tm),
        out_shape=(out, out, out),
        grid=(t // tm,),
        in_specs=[row_f32, _resident(g.shape), _resident(w_in.shape), _resident(gain.shape),
                  _resident(w_s.shape), _resident(bs_tile.shape), _resident(cs.shape)],
        out_specs=(row_half, row_half, row_half),
        compiler_params=_params(),
        name="mixer_ab_front",
    )(x, g, w_in, gain, w_s, bs_tile, cs)


def _dft_stage1_kernel(p_ref, q_ref, m_ref, t_ref):
    z = jnp.concatenate([p_ref[...], q_ref[...]], axis=0)
    t_ref[...] = _dot(m_ref[...], z).astype(BF16)


def _dft_stage2_kernel(t_ref, g_ref, o_ref, *, kb, n2):
    for j in range(kb):
        rhs = t_ref[:, j].reshape(2 * n2, D_B)
        o_ref[:, j * D_B:(j + 1) * D_B] = _dot(g_ref[j], rhs).astype(BF16)


def _position_dft(p, q, batch, seq):
    n2 = DFT_N2
    n1 = seq // n2
    cols = n2 * D_B
    m1 = jnp.asarray(_stage1_dft(n1)).astype(BF16)
    g2 = jnp.asarray(_stage2_dft(n1, n2)).astype(BF16)
    col_blk = min(cols, DFT_COL_BLOCK_BYTES // (2 * n1))
    p3 = p.reshape(batch, n1, cols)
    q3 = q.reshape(batch, n1, cols)
    in_blk = pl.BlockSpec((None, n1, col_blk), lambda b, c: (b, 0, c))
    t = pl.pallas_call(
        _dft_stage1_kernel,
        out_shape=jax.ShapeDtypeStruct((batch, 2 * n1, cols), BF16),
        grid=(batch, cols // col_blk),
        in_specs=[in_blk, in_blk, _resident(m1.shape)],
        out_specs=pl.BlockSpec((None, 2 * n1, col_blk), lambda b, c: (b, 0, c)),
        compiler_params=_params(2),
        name="dft_stage1",
    )(p3, q3, m1)
    kb = min(n1, DFT_K1_BLOCK)
    t5 = t.reshape(batch, 2, n1, n2, D_B)
    out = pl.pallas_call(
        functools.partial(_dft_stage2_kernel, kb=kb, n2=n2),
        out_shape=jax.ShapeDtypeStruct((batch, n2, n1 * D_B), BF16),
        grid=(batch, n1 // kb),
        in_specs=[pl.BlockSpec((None, 2, kb, n2, D_B), lambda b, k: (b, 0, k, 0, 0)),
                  pl.BlockSpec((kb, n2, 2 * n2), lambda b, k: (k, 0, 0))],
        out_specs=pl.BlockSpec((None, n2, kb * D_B), lambda b, k: (b, 0, k)),
        compiler_params=_params(2),
        name="dft_stage2",
    )(t5, g2)
    return out.reshape(batch * seq, D_B)


def _mixer_ab_out_kernel(x_ref, a_ref, ff_ref, w_ref, o_ref):
    mix = jnp.concatenate([a_ref[...], ff_ref[...]], axis=-1)
    o_ref[...] = x_ref[...] + _dot(mix, w_ref[...])


def _mixer_ab_out(x, a, ff, w_out, *, tm):
    t = x.shape[0]
    row_f32 = pl.BlockSpec((tm, D_MODEL), lambda i: (i, 0))
    row_half = pl.BlockSpec((tm, D_A), lambda i: (i, 0))
    return pl.pallas_call(
        _mixer_ab_out_kernel,
        out_shape=jax.ShapeDtypeStruct((t, D_MODEL), F32),
        grid=(t // tm,),
        in_specs=[row_f32, row_half, row_half, _resident(w_out.shape)],
        out_specs=row_f32,
        compiler_params=_params(),
        name="mixer_ab_out",
    )(x, a, ff, w_out)


def _halo_specs(tm, n_tiles):
    per = tm // HALO
    last = n_tiles * per - 1
    prev = pl.BlockSpec((HALO, D_MODEL), lambda i: (jnp.maximum(i * per - 1, 0), 0))
    main = pl.BlockSpec((tm, D_MODEL), lambda i: (i, 0))
    nxt = pl.BlockSpec((HALO, D_MODEL), lambda i: (jnp.minimum((i + 1) * per, last), 0))
    return prev, main, nxt


def _normed_tile_with_halo(xp_ref, x_ref, xn_ref, g_ref, h_scr, *, tm, seq_tiles):
    pos = lax.rem(pl.program_id(0), seq_tiles)
    g = g_ref[...]
    hp = jnp.where(pos != 0, _rmsnorm(xp_ref[...], g), 0.0)
    hn = jnp.where(pos != seq_tiles - 1, _rmsnorm(xn_ref[...], g), 0.0)
    h_scr[0:HALO, :] = hp.astype(BF16)
    h_scr[HALO:HALO + tm, :] = _rmsnorm(x_ref[...], g).astype(BF16)
    h_scr[HALO + tm:, :] = hn.astype(BF16)


def _dwconv3_rows(p_ref, w, b, tm):
    return (p_ref[HALO - 1:HALO - 1 + tm, :] * w[0:1, :]
            + p_ref[HALO:HALO + tm, :] * w[1:2, :]
            + p_ref[HALO + 1:HALO + 1 + tm, :] * w[2:3, :] + b)


def _conv_ffn_kernel(xp_ref, x_ref, xn_ref, g_ref, wup_ref, cw_ref, cb_ref, wdn_ref, fin_ref,
                     o_ref, h_scr, pg_scr, pv_scr, act_scr, *, tm, seq_tiles, final):
    _normed_tile_with_halo(xp_ref, x_ref, xn_ref, g_ref, h_scr, tm=tm, seq_tiles=seq_tiles)
    h = h_scr[...]
    for c in range(D_FF // FF_COLS):
        slot = c % 2
        gcols = slice(c * FF_COLS, (c + 1) * FF_COLS)
        vcols = slice(D_FF + c * FF_COLS, D_FF + (c + 1) * FF_COLS)
        pg_scr[slot] = _dot(h, wup_ref[:, gcols])
        pv_scr[slot] = _dot(h, wup_ref[:, vcols])
        gate = _dwconv3_rows(pg_scr.at[slot], cw_ref[:, gcols], cb_ref[:, gcols], tm)
        val = _dwconv3_rows(pv_scr.at[slot], cw_ref[:, vcols], cb_ref[:, vcols], tm)
        act_scr[:, gcols] = (_silu(gate) * val).astype(BF16)
    y = x_ref[...] + _dot(act_scr[...], wdn_ref[...])
    if final:
        y = _rmsnorm(y, fin_ref[...])
    o_ref[...] = y


def _conv_ffn(x, g, w_up, conv_w, conv_b, w_down, fin, *, tm, seq, final):
    t = x.shape[0]
    n_tiles = t // tm
    prev, main, nxt = _halo_specs(tm, n_tiles)
    rows = tm + 2 * HALO
    return pl.pallas_call(
        functools.partial(_conv_ffn_kernel, tm=tm, seq_tiles=seq // tm, final=final),
        out_shape=jax.ShapeDtypeStruct((t, D_MODEL), F32),
        grid=(n_tiles,),
        in_specs=[prev, main, nxt, _resident(g.shape), _resident(w_up.shape),
                  _resident(conv_w.shape), _resident(conv_b.shape), _resident(w_down.shape),
                  _resident(fin.shape)],
        out_specs=main,
        scratch_shapes=[pltpu.VMEM((rows, D_MODEL), BF16),
                        pltpu.VMEM((2, rows, FF_COLS), F32),
                        pltpu.VMEM((2, rows, FF_COLS), F32),
                        pltpu.VMEM((tm, D_FF), BF16)],
        compiler_params=_params(),
        name="conv_ffn",
    )(x, x, x, g, w_up, conv_w, conv_b, w_down, fin)


def _mixer_c_kernel(xp_ref, x_ref, xn_ref, g_ref, win_ref, cw_ref, cb_ref, wout_ref,
                    o_ref, h_scr, cz_scr, y_scr, *, tm, seq_tiles):
    _normed_tile_with_halo(xp_ref, x_ref, xn_ref, g_ref, h_scr, tm=tm, seq_tiles=seq_tiles)
    h = h_scr[...]
    h_mid = h_scr[HALO:HALO + tm, :]
    for c in range(D_C // FF_COLS):
        slot = c % 2
        cols = slice(c * FF_COLS, (c + 1) * FF_COLS)
        gate_b = _dot(h_mid, _proj_cols(win_ref, 0, c))
        gate_c = _dot(h, _proj_cols(win_ref, 1, c))
        z = _dot(h, _proj_cols(win_ref, 2, c))
        cz_scr[slot] = gate_c * z
        conv = _dwconv3_rows(cz_scr.at[slot], cw_ref[:, cols], cb_ref[:, cols], tm)
        y_scr[:, cols] = (gate_b * conv).astype(BF16)
    o_ref[...] = x_ref[...] + _dot(y_scr[...], wout_ref[...])


def _proj_cols(w_ref, part, c):
    start = part * D_C + c * FF_COLS
    return w_ref[:, start:start + FF_COLS]


def _mixer_c(x, g, w_in, conv_w, conv_b, w_out, *, tm, seq):
    t = x.shape[0]
    n_tiles = t // tm
    prev, main, nxt = _halo_specs(tm, n_tiles)
    rows = tm + 2 * HALO
    return pl.pallas_call(
        functools.partial(_mixer_c_kernel, tm=tm, seq_tiles=seq // tm),
        out_shape=jax.ShapeDtypeStruct((t, D_MODEL), F32),
        grid=(n_tiles,),
        in_specs=[prev, main, nxt, _resident(g.shape), _resident(w_in.shape),
                  _resident(conv_w.shape), _resident(conv_b.shape), _resident(w_out.shape)],
        out_specs=main,
        scratch_shapes=[pltpu.VMEM((rows, D_MODEL), BF16),
                        pltpu.VMEM((2, rows, FF_COLS), F32),
                        pltpu.VMEM((tm, D_C), BF16)],
        compiler_params=_params(),
        name="mixer_c",
    )(x, x, x, g, w_in, conv_w, conv_b, w_out)


def _trunk(x3, wts):
    batch, seq, _ = x3.shape
    tm = min(TOKEN_TILE, seq)
    assert seq % tm == 0 and tm % CHUNK == 0 and seq % DFT_N2 == 0
    x = x3.reshape(batch * seq, D_MODEL)
    depth = wts["norm_mix"].shape[0]
    for l in range(depth):
        i = l // 2
        g_mix = wts["norm_mix"][l][None, :]
        if l % 2 == 0:
            a, p, q = _mixer_ab_front(x, g_mix, wts["w_in_ab"][i], wts["sgu_gain"][i],
                                      wts["w_s"][i], wts["bs_tile"][i], wts["cs"], tm=tm)
            ff = _position_dft(p, q, batch, seq)
            x = _mixer_ab_out(x, a, ff, wts["w_out_ab"][i], tm=tm)
        else:
            x = _mixer_c(x, g_mix, wts["w_in_c"][i], wts["conv_w_c"][i], wts["conv_b_c"][i],
                         wts["w_out_c"][i], tm=tm, seq=seq)
        x = _conv_ffn(x, wts["norm_ffn"][l][None, :], wts["w_up"][l], wts["ffn_conv_w"][l],
                      wts["ffn_conv_b"][l], wts["w_down"][l], wts["final_norm"],
                      tm=tm, seq=seq, final=(l == depth - 1))
    return x.reshape(batch, seq, D_MODEL)


def kernel(x_prompt, x_sample, norm_mix, w_in_ab, sgu_gain, w_s, b_s, w_out_ab, w_in_c, conv_w_c,
           conv_b_c, w_out_c, norm_ffn, w_up, ffn_conv_w, ffn_conv_b, w_down, final_norm):
    n_even = w_in_ab.shape[0]
    wts = dict(
        norm_mix=norm_mix, norm_ffn=norm_ffn, final_norm=final_norm[None, :],
        w_in_ab=w_in_ab.astype(BF16), w_out_ab=w_out_ab.astype(BF16),
        w_s=w_s.astype(BF16), sgu_gain=sgu_gain.reshape(n_even, 1, D_A),
        bs_tile=jnp.repeat(jnp.swapaxes(b_s, 1, 2), A_HEAD_DIM, axis=2),
        cs=jnp.asarray(_channel_dft()).astype(BF16),
        w_in_c=w_in_c.astype(BF16), w_out_c=w_out_c.astype(BF16),
        conv_w_c=conv_w_c, conv_b_c=conv_b_c[:, None, :],
        w_up=w_up.astype(BF16), w_down=w_down.astype(BF16),
        ffn_conv_w=ffn_conv_w, ffn_conv_b=ffn_conv_b[:, None, :],
    )
    return _trunk(x_prompt, wts), _trunk(x_sample, wts)
```

```python
import functools

import numpy as np
import jax
import jax.numpy as jnp
from jax import lax
from jax.experimental import pallas as pl
from jax.experimental.pallas import tpu as pltpu

D_MODEL = 1024
CHUNK = 128
D_A = D_MODEL // 2
A_HEAD_DIM = 128
A_HEADS = D_A // A_HEAD_DIM
D_B = D_MODEL // 2
B_GROUP_DIM = 128
B_GROUPS = D_B // B_GROUP_DIM
D_C = D_MODEL
D_FF = ((8 * D_MODEL // 3 + 127) // 128) * 128
EPS = 1e-6

F32 = jnp.float32
BF16 = jnp.bfloat16

V7X_VMEM_BYTES = 64 * 1024 * 1024
V7X_MXU_DIM = 256
LANES = 128
F32_SUBLANES = 8
BF16_SUBLANES = 16

HALO = F32_SUBLANES
TOKEN_TILE = 512
FF_COLS = V7X_MXU_DIM
DFT_N2 = 128
DFT_COL_BLOCK_BYTES = 2 * 1024 * 1024
DFT_K1_BLOCK = 16
VMEM_LIMIT = V7X_VMEM_BYTES - 8 * 1024 * 1024


def _params(n_axes=1, flags=None):
    return pltpu.CompilerParams(
        dimension_semantics=("parallel",) * n_axes, vmem_limit_bytes=VMEM_LIMIT, flags=flags)


def _resident(shape):
    zeros = (0,) * len(shape)
    return pl.BlockSpec(shape, lambda *_: zeros, pipeline_mode=pl.Buffered(1))


def _rmsnorm(x, g):
    return x * lax.rsqrt(jnp.mean(x * x, axis=-1, keepdims=True) + EPS) * g


def _gelu_tanh(x):
    return x * (0.5 * (1.0 + jnp.tanh(0.7978845608028654 * (x + 0.044715 * (x * x * x)))))


def _silu(x):
    hx = 0.5 * x
    return hx + hx * jnp.tanh(hx)


def _dot(a, b):
    return jnp.dot(a, b, preferred_element_type=F32)


@functools.lru_cache(maxsize=None)
def _channel_dft():
    n = B_GROUP_DIM
    k = (np.arange(n)[:, None] * np.arange(n)[None, :]) % n
    ang = 2.0 * np.pi * k / n
    m = np.concatenate([np.cos(ang), np.sin(ang)], axis=1) / np.sqrt(n)
    return m.astype(np.float32)


@functools.lru_cache(maxsize=None)
def _stage1_dft(n1):
    k = (np.arange(n1)[:, None] * np.arange(n1)[None, :]) % n1
    ang = 2.0 * np.pi * k / n1
    c, s = np.cos(ang), np.sin(ang)
    m = np.block([[c, -s], [-s, -c]]) / np.sqrt(n1)
    return m.astype(np.float32)


@functools.lru_cache(maxsize=None)
def _stage2_dft(n1, n2):
    s = n1 * n2
    k1 = np.arange(n1)[:, None, None]
    k2 = np.arange(n2)[None, :, None]
    s2 = np.arange(n2)[None, None, :]
    ang = 2.0 * np.pi * ((s2 * (k1 + n1 * k2)) % s) / s
    m = np.concatenate([np.cos(ang), np.sin(ang)], axis=2) / np.sqrt(n2)
    return m.astype(np.float32)


def _mixer_ab_front_kernel(x_ref, g_ref, win_ref, gain_ref, ws_ref, bs_ref, cs_ref,
                           a_ref, p_ref, q_ref, *, tm):
    h = _rmsnorm(x_ref[...], g_ref[...]).astype(BF16)
    proj = _dot(h, win_ref[...])
    for hd in range(A_HEADS):
        lanes = slice(hd * A_HEAD_DIM, (hd + 1) * A_HEAD_DIM)
        u = _gelu_tanh(proj[:, hd * A_HEAD_DIM:(hd + 1) * A_HEAD_DIM])
        v = _gelu_tanh(proj[:, D_A + hd * A_HEAD_DIM:D_A + (hd + 1) * A_HEAD_DIM])
        vn = _rmsnorm(v, gain_ref[:, lanes]).astype(BF16)
        for n in range(tm // CHUNK):
            rows = slice(n * CHUNK, (n + 1) * CHUNK)
            s = _dot(ws_ref[hd], vn[rows, :]) + bs_ref[:, lanes]
            a_ref[rows, lanes] = (u[rows, :] * s).astype(BF16)
    for grp in range(B_GROUPS):
        lanes = slice(grp * B_GROUP_DIM, (grp + 1) * B_GROUP_DIM)
        f = proj[:, 2 * D_A + grp * B_GROUP_DIM:2 * D_A + (grp + 1) * B_GROUP_DIM]
        pq = _dot(f.astype(BF16), cs_ref[...])
        p_ref[:, lanes] = pq[:, :B_GROUP_DIM].astype(BF16)
        q_ref[:, lanes] = pq[:, B_GROUP_DIM:].astype(BF16)


def _mixer_ab_front(x, g, w_in, gain, w_s, bs_tile, cs, *, tm):
    t = x.shape[0]
    row_f32 = pl.BlockSpec((tm, D_MODEL), lambda i: (i, 0))
    row_half = pl.BlockSpec((tm, D_A), lambda i: (i, 0))
    out = jax.ShapeDtypeStruct((t, D_A), BF16)
    return pl.pallas_call(
        functools.partial(_mixer_ab_front_kernel, tm=tm),
        out_shape=(out, out, out),
        grid=(t // tm,),
        in_specs=[row_f32, _resident(g.shape), _resident(w_in.shape), _resident(gain.shape),
                  _resident(w_s.shape), _resident(bs_tile.shape), _resident(cs.shape)],
        out_specs=(row_half, row_half, row_half),
        compiler_params=_params(),
        name="mixer_ab_front",
    )(x, g, w_in, gain, w_s, bs_tile, cs)


def _dft_stage1_kernel(p_ref, q_ref, m_ref, t_ref):
    z = jnp.concatenate([p_ref[...], q_ref[...]], axis=0)
    t_ref[...] = _dot(m_ref[...], z).astype(BF16)


def _dft_stage2_kernel(t_ref, g_ref, o_ref, *, kb, n2):
    for j in range(kb):
        rhs = t_ref[:, j].reshape(2 * n2, D_B)
        o_ref[:, j * D_B:(j + 1) * D_B] = _dot(g_ref[j], rhs).astype(BF16)


def _position_dft(p, q, batch, seq):
    n2 = DFT_N2
    n1 = seq // n2
    cols = n2 * D_B
    m1 = jnp.asarray(_stage1_dft(n1)).astype(BF16)
    g2 = jnp.asarray(_stage2_dft(n1, n2)).astype(BF16)
    col_blk = min(cols, DFT_COL_BLOCK_BYTES // (2 * n1))
    p3 = p.reshape(batch, n1, cols)
    q3 = q.reshape(batch, n1, cols)
    in_blk = pl.BlockSpec((None, n1, col_blk), lambda b, c: (b, 0, c))
    t = pl.pallas_call(
        _dft_stage1_kernel,
        out_shape=jax.ShapeDtypeStruct((batch, 2 * n1, cols), BF16),
        grid=(batch, cols // col_blk),
        in_specs=[in_blk, in_blk, _resident(m1.shape)],
        out_specs=pl.BlockSpec((None, 2 * n1, col_blk), lambda b, c: (b, 0, c)),
        compiler_params=_params(2),
        name="dft_stage1",
    )(p3, q3, m1)
    kb = min(n1, DFT_K1_BLOCK)
    t5 = t.reshape(batch, 2, n1, n2, D_B)
    out = pl.pallas_call(
        functools.partial(_dft_stage2_kernel, kb=kb, n2=n2),
        out_shape=jax.ShapeDtypeStruct((batch, n2, n1 * D_B), BF16),
        grid=(batch, n1 // kb),
        in_specs=[pl.BlockSpec((None, 2, kb, n2, D_B), lambda b, k: (b, 0, k, 0, 0)),
                  pl.BlockSpec((kb, n2, 2 * n2), lambda b, k: (k, 0, 0))],
        out_specs=pl.BlockSpec((None, n2, kb * D_B), lambda b, k: (b, 0, k)),
        compiler_params=_params(2),
        name="dft_stage2",
    )(t5, g2)
    return out.reshape(batch * seq, D_B)


def _mixer_ab_out_kernel(x_ref, a_ref, ff_ref, w_ref, o_ref):
    mix = jnp.concatenate([a_ref[...], ff_ref[...]], axis=-1)
    o_ref[...] = x_ref[...] + _dot(mix, w_ref[...])


def _mixer_ab_out(x, a, ff, w_out, *, tm):
    t = x.shape[0]
    row_f32 = pl.BlockSpec((tm, D_MODEL), lambda i: (i, 0))
    row_half = pl.BlockSpec((tm, D_A), lambda i: (i, 0))
    return pl.pallas_call(
        _mixer_ab_out_kernel,
        out_shape=jax.ShapeDtypeStruct((t, D_MODEL), F32),
        grid=(t // tm,),
        in_specs=[row_f32, row_half, row_half, _resident(w_out.shape)],
        out_specs=row_f32,
        compiler_params=_params(),
        name="mixer_ab_out",
    )(x, a, ff, w_out)


def _halo_specs(tm, n_tiles):
    per = tm // HALO
    last = n_tiles * per - 1
    prev = pl.BlockSpec((HALO, D_MODEL), lambda i: (jnp.maximum(i * per - 1, 0), 0))
    main = pl.BlockSpec((tm, D_MODEL), lambda i: (i, 0))
    nxt = pl.BlockSpec((HALO, D_MODEL), lambda i: (jnp.minimum((i + 1) * per, last), 0))
    return prev, main, nxt


def _normed_tile_with_halo(xp_ref, x_ref, xn_ref, g_ref, h_scr, *, tm, seq_tiles):
    pos = lax.rem(pl.program_id(0), seq_tiles)
    g = g_ref[...]
    hp = jnp.where(pos != 0, _rmsnorm(xp_ref[...], g), 0.0)
    hn = jnp.where(pos != seq_tiles - 1, _rmsnorm(xn_ref[...], g), 0.0)
    h_scr[...] = jnp.concatenate([hp, _rmsnorm(x_ref[...], g), hn], axis=0).astype(BF16)


def _conv_scratch(rows, slots=2):
    return pltpu.VMEM((slots, FF_COLS // LANES, rows, LANES), F32)


def _store_slabs(scr, slot, value):
    for s in range(FF_COLS // LANES):
        scr[slot, s] = value[:, s * LANES:(s + 1) * LANES]


def _dwconv3_rows(scr, slot, w, b, tm):
    out = []
    for s in range(FF_COLS // LANES):
        ws = w[:, s * LANES:(s + 1) * LANES]
        out.append(scr[slot, s, pl.ds(HALO - 1, tm), :] * ws[0:1, :]
                   + scr[slot, s, pl.ds(HALO, tm), :] * ws[1:2, :]
                   + scr[slot, s, pl.ds(HALO + 1, tm), :] * ws[2:3, :]
                   + b[:, s * LANES:(s + 1) * LANES])
    return jnp.concatenate(out, axis=-1)


def _conv_ffn_kernel(xp_ref, x_ref, xn_ref, g_ref, wup_ref, cw_ref, cb_ref, wdn_ref, fin_ref,
                     o_ref, h_scr, pg_scr, pv_scr, *, tm, seq_tiles, final):
    _normed_tile_with_halo(xp_ref, x_ref, xn_ref, g_ref, h_scr, tm=tm, seq_tiles=seq_tiles)
    h = h_scr[...]
    n_blocks = D_FF // FF_COLS

    def gate_cols(c):
        return slice(c * FF_COLS, (c + 1) * FF_COLS)

    def val_cols(c):
        return slice(D_FF + c * FF_COLS, D_FF + (c + 1) * FF_COLS)

    def up_project(c):
        _store_slabs(pg_scr, c % 2, _dot(h, wup_ref[:, gate_cols(c)]))
        _store_slabs(pv_scr, c % 2, _dot(h, wup_ref[:, val_cols(c)]))

    o_ref[...] = x_ref[...]
    up_project(0)
    for c in range(n_blocks):
        if c + 1 < n_blocks:
            up_project(c + 1)
        gate = _dwconv3_rows(pg_scr, c % 2, cw_ref[:, gate_cols(c)], cb_ref[:, gate_cols(c)], tm)
        val = _dwconv3_rows(pv_scr, c % 2, cw_ref[:, val_cols(c)], cb_ref[:, val_cols(c)], tm)
        act = (_silu(gate) * val).astype(BF16)
        o_ref[...] += _dot(act, wdn_ref[gate_cols(c), :])
    if final:
        o_ref[...] = _rmsnorm(o_ref[...], fin_ref[...])


def _conv_ffn(x, g, w_up, conv_w, conv_b, w_down, fin, *, tm, seq, final):
    t = x.shape[0]
    n_tiles = t // tm
    prev, main, nxt = _halo_specs(tm, n_tiles)
    rows = tm + 2 * HALO
    return pl.pallas_call(
        functools.partial(_conv_ffn_kernel, tm=tm, seq_tiles=seq // tm, final=final),
        out_shape=jax.ShapeDtypeStruct((t, D_MODEL), F32),
        grid=(n_tiles,),
        in_specs=[prev, main, nxt, _resident(g.shape), _resident(w_up.shape),
                  _resident(conv_w.shape), _resident(conv_b.shape), _resident(w_down.shape),
                  _resident(fin.shape)],
        out_specs=main,
        scratch_shapes=[pltpu.VMEM((rows, D_MODEL), BF16),
                        _conv_scratch(rows), _conv_scratch(rows)],
        compiler_params=_params(),
        name="conv_ffn",
    )(x, x, x, g, w_up, conv_w, conv_b, w_down, fin)


def _mixer_c_kernel(xp_ref, x_ref, xn_ref, g_ref, win_ref, cw_ref, cb_ref, wout_ref,
                    o_ref, h_scr, cz_scr, y_scr, *, tm, seq_tiles):
    _normed_tile_with_halo(xp_ref, x_ref, xn_ref, g_ref, h_scr, tm=tm, seq_tiles=seq_tiles)
    h = h_scr[...]
    for c in range(D_C // FF_COLS):
        slot = c % 2
        cols = slice(c * FF_COLS, (c + 1) * FF_COLS)
        gate_b = _dot(h, _proj_cols(win_ref, 0, c))[HALO:HALO + tm, :]
        gate_c = _dot(h, _proj_cols(win_ref, 1, c))
        z = _dot(h, _proj_cols(win_ref, 2, c))
        _store_slabs(cz_scr, slot, gate_c * z)
        conv = _dwconv3_rows(cz_scr, slot, cw_ref[:, cols], cb_ref[:, cols], tm)
        y_scr[:, cols] = (gate_b * conv).astype(BF16)
    o_ref[...] = x_ref[...] + _dot(y_scr[...], wout_ref[...])


def _proj_cols(w_ref, part, c):
    start = part * D_C + c * FF_COLS
    return w_ref[:, start:start + FF_COLS]


def _mixer_c(x, g, w_in, conv_w, conv_b, w_out, *, tm, seq):
    t = x.shape[0]
    n_tiles = t // tm
    prev, main, nxt = _halo_specs(tm, n_tiles)
    rows = tm + 2 * HALO
    return pl.pallas_call(
        functools.partial(_mixer_c_kernel, tm=tm, seq_tiles=seq // tm),
        out_shape=jax.ShapeDtypeStruct((t, D_MODEL), F32),
        grid=(n_tiles,),
        in_specs=[prev, main, nxt, _resident(g.shape), _resident(w_in.shape),
                  _resident(conv_w.shape), _resident(conv_b.shape), _resident(w_out.shape)],
        out_specs=main,
        scratch_shapes=[pltpu.VMEM((rows, D_MODEL), BF16),
                        _conv_scratch(rows),
                        pltpu.VMEM((tm, D_C), BF16)],
        compiler_params=_params(),
        name="mixer_c",
    )(x, x, x, g, w_in, conv_w, conv_b, w_out)


def _trunk(x3, wts):
    batch, seq, _ = x3.shape
    tm = min(TOKEN_TILE, seq)
    assert seq % tm == 0 and tm % CHUNK == 0 and seq % DFT_N2 == 0
    x = x3.reshape(batch * seq, D_MODEL)
    depth = wts["norm_mix"].shape[0]
    for l in range(depth):
        i = l // 2
        g_mix = wts["norm_mix"][l][None, :]
        if l % 2 == 0:
            a, p, q = _mixer_ab_front(x, g_mix, wts["w_in_ab"][i], wts["sgu_gain"][i],
                                      wts["w_s"][i], wts["bs_tile"][i], wts["cs"], tm=tm)
            ff = _position_dft(p, q, batch, seq)
            x = _mixer_ab_out(x, a, ff, wts["w_out_ab"][i], tm=tm)
        else:
            x = _mixer_c(x, g_mix, wts["w_in_c"][i], wts["conv_w_c"][i], wts["conv_b_c"][i],
                         wts["w_out_c"][i], tm=tm, seq=seq)
        x = _conv_ffn(x, wts["norm_ffn"][l][None, :], wts["w_up"][l], wts["ffn_conv_w"][l],
                      wts["ffn_conv_b"][l], wts["w_down"][l], wts["final_norm"],
                      tm=tm, seq=seq, final=(l == depth - 1))
    return x.reshape(batch, seq, D_MODEL)


def kernel(x_prompt, x_sample, norm_mix, w_in_ab, sgu_gain, w_s, b_s, w_out_ab, w_in_c, conv_w_c,
           conv_b_c, w_out_c, norm_ffn, w_up, ffn_conv_w, ffn_conv_b, w_down, final_norm):
    n_even = w_in_ab.shape[0]
    wts = dict(
        norm_mix=norm_mix, norm_ffn=norm_ffn, final_norm=final_norm[None, :],
        w_in_ab=w_in_ab.astype(BF16), w_out_ab=w_out_ab.astype(BF16),
        w_s=w_s.astype(BF16), sgu_gain=sgu_gain.reshape(n_even, 1, D_A),
        bs_tile=jnp.repeat(jnp.swapaxes(b_s, 1, 2), A_HEAD_DIM, axis=2),
        cs=jnp.asarray(_channel_dft()).astype(BF16),
        w_in_c=w_in_c.astype(BF16), w_out_c=w_out_c.astype(BF16),
        conv_w_c=conv_w_c, conv_b_c=conv_b_c[:, None, :],
        w_up=w_up.astype(BF16), w_down=w_down.astype(BF16),
        ffn_conv_w=ffn_conv_w, ffn_conv_b=ffn_conv_b[:, None, :],
    )
    return _trunk(x_prompt, wts), _trunk(x_sample, wts)
```

```python
import functools

import numpy as np
import jax
import jax.numpy as jnp
from jax import lax
from jax.experimental import pallas as pl
from jax.experimental.pallas import tpu as pltpu

D_MODEL = 1024
CHUNK = 128
D_A = D_MODEL // 2
A_HEAD_DIM = 128
A_HEADS = D_A // A_HEAD_DIM
D_B = D_MODEL // 2
B_GROUP_DIM = 128
B_GROUPS = D_B // B_GROUP_DIM
D_C = D_MODEL
D_FF = ((8 * D_MODEL // 3 + 127) // 128) * 128
EPS = 1e-6

F32 = jnp.float32
BF16 = jnp.bfloat16

V7X_VMEM_BYTES = 64 * 1024 * 1024
V7X_MXU_DIM = 256
LANES = 128
F32_SUBLANES = 8
BF16_SUBLANES = 16

HALO = F32_SUBLANES
MIX_HALO = BF16_SUBLANES
TOKEN_TILE = 512
FF_COLS = V7X_MXU_DIM
DFT_N2 = 128
DFT_BLOCK_BYTES = 4 * 1024 * 1024
SLAB_PAD = 4
VMEM_LIMIT = V7X_VMEM_BYTES - 8 * 1024 * 1024


def _params(n_axes=1):
    return pltpu.CompilerParams(
        dimension_semantics=("parallel",) * n_axes, vmem_limit_bytes=VMEM_LIMIT)


def _resident(shape):
    zeros = (0,) * len(shape)
    return pl.BlockSpec(shape, lambda *_: zeros, pipeline_mode=pl.Buffered(1))


def _rmsnorm(x, g):
    return x * lax.rsqrt(jnp.mean(x * x, axis=-1, keepdims=True) + EPS) * g


def _gelu_tanh(x):
    return x * (0.5 * (1.0 + jnp.tanh(0.7978845608028654 * (x + 0.044715 * (x * x * x)))))


def _silu(x):
    hx = 0.5 * x
    return hx + hx * jnp.tanh(hx)


def _dot(a, b):
    return jnp.dot(a, b, preferred_element_type=F32)


@functools.lru_cache(maxsize=None)
def _channel_dft():
    n = B_GROUP_DIM
    k = (np.arange(n)[:, None] * np.arange(n)[None, :]) % n
    ang = 2.0 * np.pi * k / n
    m = np.concatenate([np.cos(ang), np.sin(ang)], axis=1) / np.sqrt(n)
    return m.astype(np.float32)


@functools.lru_cache(maxsize=None)
def _stage1_dft(n1):
    k = (np.arange(n1)[:, None] * np.arange(n1)[None, :]) % n1
    ang = 2.0 * np.pi * k / n1
    c, s = np.cos(ang), np.sin(ang)
    m = np.block([[c, -s], [-s, -c]]) / np.sqrt(n1)
    return m.astype(np.float32)


@functools.lru_cache(maxsize=None)
def _stage2_dft(n1, n2):
    s = n1 * n2
    k1 = np.arange(n1)[:, None, None]
    k2 = np.arange(n2)[None, :, None]
    s2 = np.arange(n2)[None, None, :]
    ang = 2.0 * np.pi * ((s2 * (k1 + n1 * k2)) % s) / s
    m = np.concatenate([np.cos(ang), np.sin(ang)], axis=2) / np.sqrt(n2)
    return m.astype(np.float32)


def _mixer_ab_front_kernel(x_ref, g_ref, win_ref, gain_ref, ws_ref, bs_ref,
                           a_ref, f_ref, *, tm):
    h = _rmsnorm(x_ref[...], g_ref[...]).astype(BF16)
    proj = _dot(h, win_ref[...])
    for hd in range(A_HEADS):
        lanes = slice(hd * A_HEAD_DIM, (hd + 1) * A_HEAD_DIM)
        u = _gelu_tanh(proj[:, hd * A_HEAD_DIM:(hd + 1) * A_HEAD_DIM])
        v = _gelu_tanh(proj[:, D_A + hd * A_HEAD_DIM:D_A + (hd + 1) * A_HEAD_DIM])
        vn = _rmsnorm(v, gain_ref[:, lanes]).astype(BF16)
        for n in range(tm // CHUNK):
            rows = slice(n * CHUNK, (n + 1) * CHUNK)
            s = _dot(ws_ref[hd], vn[rows, :]) + bs_ref[:, lanes]
            a_ref[rows, lanes] = (u[rows, :] * s).astype(BF16)
    f_ref[...] = proj[:, 2 * D_A:].astype(BF16)


def _mixer_ab_front(x, g, w_in, gain, w_s, bs_tile, *, tm):
    t = x.shape[0]
    row_f32 = pl.BlockSpec((tm, D_MODEL), lambda i: (i, 0))
    row_half = pl.BlockSpec((tm, D_A), lambda i: (i, 0))
    out = jax.ShapeDtypeStruct((t, D_A), BF16)
    return pl.pallas_call(
        functools.partial(_mixer_ab_front_kernel, tm=tm),
        out_shape=(out, out),
        grid=(t // tm,),
        in_specs=[row_f32, _resident(g.shape), _resident(w_in.shape), _resident(gain.shape),
                  _resident(w_s.shape), _resident(bs_tile.shape)],
        out_specs=(row_half, row_half),
        compiler_params=_params(),
        name="mixer_ab_front",
    )(x, g, w_in, gain, w_s, bs_tile)


def _fourier_kernel(f_ref, cs_ref, m1_ref, g2_ref, o_ref, a_scr, tr_scr, ti_scr,
                    *, n1, n2, groups, jb):
    pf = n2 + SLAB_PAD
    pt = n1 + SLAB_PAD
    for grp in range(groups):
        lanes = slice(grp * B_GROUP_DIM, (grp + 1) * B_GROUP_DIM)

        def copy_in(s1, carry):
            rows = pl.ds(pl.multiple_of(s1 * n2, n2), n2)
            a_scr[pl.ds(s1 * pf, n2), :] = f_ref[rows, lanes].astype(F32)
            return carry

        def stage1(it, carry):
            j0 = it * jb
            x = jnp.concatenate([a_scr[pl.ds(j0 + t, n1, stride=pf), :] for t in range(jb)], axis=0)
            pq = _dot(x.astype(BF16), cs_ref[...]).astype(BF16)
            p = jnp.concatenate([pq[t * n1:(t + 1) * n1, :B_GROUP_DIM] for t in range(jb)], axis=1)
            q = jnp.concatenate([pq[t * n1:(t + 1) * n1, B_GROUP_DIM:] for t in range(jb)], axis=1)
            t_all = _dot(m1_ref[...], jnp.concatenate([p, q], axis=0))
            for t in range(jb):
                cols = slice(t * B_GROUP_DIM, (t + 1) * B_GROUP_DIM)
                tr_scr[pl.ds((j0 + t) * pt, n1), :] = t_all[:n1, cols]
                ti_scr[pl.ds((j0 + t) * pt, n1), :] = t_all[n1:, cols]
            return carry

        def stage2(k1, carry):
            tr = tr_scr[pl.ds(k1, n2, stride=pt), :]
            ti = ti_scr[pl.ds(k1, n2, stride=pt), :]
            rhs = jnp.concatenate([tr, ti], axis=0).astype(BF16)
            a_scr[pl.ds(k1, n2, stride=pt), :] = _dot(g2_ref[k1], rhs)
            return carry

        def copy_out(k2, carry):
            rows = pl.ds(pl.multiple_of(k2 * n1, n1), n1)
            o_ref[rows, lanes] = a_scr[pl.ds(k2 * pt, n1), :].astype(BF16)
            return carry

        lax.fori_loop(0, n1, copy_in, 0, unroll=min(n1, 8))
        lax.fori_loop(0, n2 // jb, stage1, 0, unroll=min(n2 // jb, 4))
        lax.fori_loop(0, n1, stage2, 0, unroll=4)
        lax.fori_loop(0, n2, copy_out, 0, unroll=8)


def _fourier_mix(f, cs, batch, seq):
    n2 = DFT_N2
    n1 = seq // n2
    jb = V7X_MXU_DIM // n1
    assert n1 * n2 == seq and n1 % BF16_SUBLANES == 0 and V7X_MXU_DIM % n1 == 0 and n2 % jb == 0
    groups = B_GROUPS if seq * D_B * 2 <= DFT_BLOCK_BYTES else 1
    m1 = jnp.asarray(_stage1_dft(n1)).astype(BF16)
    g2 = jnp.asarray(_stage2_dft(n1, n2)).astype(BF16)
    blk = pl.BlockSpec((seq, groups * B_GROUP_DIM), lambda b, g: (b, g))
    slab_rows = max(n1 * (n2 + SLAB_PAD), n2 * (n1 + SLAB_PAD))
    return pl.pallas_call(
        functools.partial(_fourier_kernel, n1=n1, n2=n2, groups=groups, jb=jb),
        out_shape=jax.ShapeDtypeStruct((batch * seq, D_B), BF16),
        grid=(batch, B_GROUPS // groups),
        in_specs=[blk, _resident(cs.shape), _resident(m1.shape), _resident(g2.shape)],
        out_specs=blk,
        scratch_shapes=[pltpu.VMEM((slab_rows, B_GROUP_DIM), F32),
                        pltpu.VMEM((n2 * (n1 + SLAB_PAD), B_GROUP_DIM), F32),
                        pltpu.VMEM((n2 * (n1 + SLAB_PAD), B_GROUP_DIM), F32)],
        compiler_params=_params(2),
        name="fourier_mix",
    )(f, cs, m1, g2)


def _halo_specs(tm, n_tiles, halo, width):
    per = tm // halo
    last = n_tiles * per - 1
    prev = pl.BlockSpec((halo, width), lambda i: (jnp.maximum(i * per - 1, 0), 0))
    main = pl.BlockSpec((tm, width), lambda i: (i, 0))
    nxt = pl.BlockSpec((halo, width), lambda i: (jnp.minimum((i + 1) * per, last), 0))
    return prev, main, nxt


def _store_normed_with_halo(h_scr, before, tile, after, g, *, seq_tiles):
    pos = lax.rem(pl.program_id(0), seq_tiles)
    hp = jnp.where(pos != 0, _rmsnorm(before, g), 0.0)
    hn = jnp.where(pos != seq_tiles - 1, _rmsnorm(after, g), 0.0)
    h_scr[...] = jnp.concatenate([hp, _rmsnorm(tile, g), hn], axis=0).astype(BF16)


def _conv_scratch(rows):
    return pltpu.VMEM((2, FF_COLS // LANES, rows, LANES), F32)


def _store_slabs(scr, slot, value):
    for s in range(FF_COLS // LANES):
        scr[slot, s] = value[:, s * LANES:(s + 1) * LANES]


def _dwconv3_rows(scr, slot, w, b, tm):
    out = []
    for s in range(FF_COLS // LANES):
        ws = w[:, s * LANES:(s + 1) * LANES]
        out.append(scr[slot, s, pl.ds(HALO - 1, tm), :] * ws[0:1, :]
                   + scr[slot, s, pl.ds(HALO, tm), :] * ws[1:2, :]
                   + scr[slot, s, pl.ds(HALO + 1, tm), :] * ws[2:3, :]
                   + b[:, s * LANES:(s + 1) * LANES])
    return jnp.concatenate(out, axis=-1)


def _ffn_pipeline(h_scr, pg_scr, pv_scr, wup_ref, cw_ref, cb_ref, wdn_ref, o_ref, *, tm):
    h = h_scr[...]
    n_blocks = D_FF // FF_COLS

    def gate_cols(c):
        return slice(c * FF_COLS, (c + 1) * FF_COLS)

    def val_cols(c):
        return slice(D_FF + c * FF_COLS, D_FF + (c + 1) * FF_COLS)

    def up_project(c):
        _store_slabs(pg_scr, c % 2, _dot(h, wup_ref[:, gate_cols(c)]))
        _store_slabs(pv_scr, c % 2, _dot(h, wup_ref[:, val_cols(c)]))

    up_project(0)
    for c in range(n_blocks):
        if c + 1 < n_blocks:
            up_project(c + 1)
        gate = _dwconv3_rows(pg_scr, c % 2, cw_ref[:, gate_cols(c)], cb_ref[:, gate_cols(c)], tm)
        val = _dwconv3_rows(pv_scr, c % 2, cw_ref[:, val_cols(c)], cb_ref[:, val_cols(c)], tm)
        act = (_silu(gate) * val).astype(BF16)
        o_ref[...] += _dot(act, wdn_ref[gate_cols(c), :])


def _conv_ffn_kernel(xp_ref, x_ref, xn_ref, g_ref, wup_ref, cw_ref, cb_ref, wdn_ref, fin_ref,
                     o_ref, h_scr, pg_scr, pv_scr, *, tm, seq_tiles, final):
    _store_normed_with_halo(h_scr, xp_ref[...], x_ref[...], xn_ref[...], g_ref[...],
                            seq_tiles=seq_tiles)
    o_ref[...] = x_ref[...]
    _ffn_pipeline(h_scr, pg_scr, pv_scr, wup_ref, cw_ref, cb_ref, wdn_ref, o_ref, tm=tm)
    if final:
        o_ref[...] = _rmsnorm(o_ref[...], fin_ref[...])


def _mix_out_conv_ffn_kernel(xp_ref, x_ref, xn_ref, ap_ref, a_ref, an_ref, fp_ref, f_ref, fn_ref,
                             wmix_ref, g_ref, wup_ref, cw_ref, cb_ref, wdn_ref,
                             o_ref, h_scr, pg_scr, pv_scr, *, tm, seq_tiles):
    mix = jnp.concatenate(
        [jnp.concatenate([ap_ref[...], a_ref[...], an_ref[...]], axis=0),
         jnp.concatenate([fp_ref[...], f_ref[...], fn_ref[...]], axis=0)], axis=1)
    x_all = (jnp.concatenate([xp_ref[...], x_ref[...], xn_ref[...]], axis=0)
             + _dot(mix, wmix_ref[...]))
    lo = MIX_HALO - HALO
    _store_normed_with_halo(h_scr, x_all[lo:MIX_HALO], x_all[MIX_HALO:MIX_HALO + tm],
                            x_all[MIX_HALO + tm:MIX_HALO + tm + HALO], g_ref[...],
                            seq_tiles=seq_tiles)
    o_ref[...] = x_all[MIX_HALO:MIX_HALO + tm]
    _ffn_pipeline(h_scr, pg_scr, pv_scr, wup_ref, cw_ref, cb_ref, wdn_ref, o_ref, tm=tm)


def _ffn_scratch(tm):
    rows = tm + 2 * HALO
    return [pltpu.VMEM((rows, D_MODEL), BF16), _conv_scratch(rows), _conv_scratch(rows)]


def _conv_ffn(x, g, w_up, conv_w, conv_b, w_down, fin, *, tm, seq, final):
    t = x.shape[0]
    n_tiles = t // tm
    prev, main, nxt = _halo_specs(tm, n_tiles, HALO, D_MODEL)
    return pl.pallas_call(
        functools.partial(_conv_ffn_kernel, tm=tm, seq_tiles=seq // tm, final=final),
        out_shape=jax.ShapeDtypeStruct((t, D_MODEL), F32),
        grid=(n_tiles,),
        in_specs=[prev, main, nxt, _resident(g.shape), _resident(w_up.shape),
                  _resident(conv_w.shape), _resident(conv_b.shape), _resident(w_down.shape),
                  _resident(fin.shape)],
        out_specs=main,
        scratch_shapes=_ffn_scratch(tm),
        compiler_params=_params(),
        name="conv_ffn",
    )(x, x, x, g, w_up, conv_w, conv_b, w_down, fin)


def _mix_out_conv_ffn(x, a, ff, w_mix, g, w_up, conv_w, conv_b, w_down, *, tm, seq):
    t = x.shape[0]
    n_tiles = t // tm
    x_specs = _halo_specs(tm, n_tiles, MIX_HALO, D_MODEL)
    half_specs = _halo_specs(tm, n_tiles, MIX_HALO, D_A)
    return pl.pallas_call(
        functools.partial(_mix_out_conv_ffn_kernel, tm=tm, seq_tiles=seq // tm),
        out_shape=jax.ShapeDtypeStruct((t, D_MODEL), F32),
        grid=(n_tiles,),
        in_specs=[*x_specs, *half_specs, *half_specs, _resident(w_mix.shape), _resident(g.shape),
                  _resident(w_up.shape), _resident(conv_w.shape), _resident(conv_b.shape),
                  _resident(w_down.shape)],
        out_specs=x_specs[1],
        scratch_shapes=_ffn_scratch(tm),
        compiler_params=_params(),
        name="mix_out_conv_ffn",
    )(x, x, x, a, a, a, ff, ff, ff, w_mix, g, w_up, conv_w, conv_b, w_down)


def _mixer_c_kernel(xp_ref, x_ref, xn_ref, g_ref, win_ref, cw_ref, cb_ref, wout_ref,
                    o_ref, h_scr, cz_scr, y_scr, *, tm, seq_tiles):
    _store_normed_with_halo(h_scr, xp_ref[...], x_ref[...], xn_ref[...], g_ref[...],
                            seq_tiles=seq_tiles)
    h = h_scr[...]
    for c in range(D_C // FF_COLS):
        slot = c % 2
        cols = slice(c * FF_COLS, (c + 1) * FF_COLS)
        gate_b = _dot(h, _proj_cols(win_ref, 0, c))[HALO:HALO + tm, :]
        gate_c = _dot(h, _proj_cols(win_ref, 1, c))
        z = _dot(h, _proj_cols(win_ref, 2, c))
        _store_slabs(cz_scr, slot, gate_c * z)
        conv = _dwconv3_rows(cz_scr, slot, cw_ref[:, cols], cb_ref[:, cols], tm)
        y_scr[:, cols] = (gate_b * conv).astype(BF16)
    o_ref[...] = x_ref[...] + _dot(y_scr[...], wout_ref[...])


def _proj_cols(w_ref, part, c):
    start = part * D_C + c * FF_COLS
    return w_ref[:, start:start + FF_COLS]


def _mixer_c(x, g, w_in, conv_w, conv_b, w_out, *, tm, seq):
    t = x.shape[0]
    n_tiles = t // tm
    prev, main, nxt = _halo_specs(tm, n_tiles, HALO, D_MODEL)
    rows = tm + 2 * HALO
    return pl.pallas_call(
        functools.partial(_mixer_c_kernel, tm=tm, seq_tiles=seq // tm),
        out_shape=jax.ShapeDtypeStruct((t, D_MODEL), F32),
        grid=(n_tiles,),
        in_specs=[prev, main, nxt, _resident(g.shape), _resident(w_in.shape),
                  _resident(conv_w.shape), _resident(conv_b.shape), _resident(w_out.shape)],
        out_specs=main,
        scratch_shapes=[pltpu.VMEM((rows, D_MODEL), BF16),
                        _conv_scratch(rows),
                        pltpu.VMEM((tm, D_C), BF16)],
        compiler_params=_params(),
        name="mixer_c",
    )(x, x, x, g, w_in, conv_w, conv_b, w_out)


def _trunk(x3, wts):
    batch, seq, _ = x3.shape
    tm = min(TOKEN_TILE, seq)
    assert seq % tm == 0 and tm % CHUNK == 0 and seq % DFT_N2 == 0
    x = x3.reshape(batch * seq, D_MODEL)
    depth = wts["norm_mix"].shape[0]
    assert depth % 2 == 0
    for l in range(depth):
        i = l // 2
        g_mix = wts["norm_mix"][l][None, :]
        ffn = (wts["norm_ffn"][l][None, :], wts["w_up"][l], wts["ffn_conv_w"][l],
               wts["ffn_conv_b"][l], wts["w_down"][l])
        if l % 2 == 0:
            a, f = _mixer_ab_front(x, g_mix, wts["w_in_ab"][i], wts["sgu_gain"][i],
                                   wts["w_s"][i], wts["bs_tile"][i], tm=tm)
            ff = _fourier_mix(f, wts["cs"], batch, seq)
            x = _mix_out_conv_ffn(x, a, ff, wts["w_out_ab"][i], *ffn, tm=tm, seq=seq)
        else:
            x = _mixer_c(x, g_mix, wts["w_in_c"][i], wts["conv_w_c"][i], wts["conv_b_c"][i],
                         wts["w_out_c"][i], tm=tm, seq=seq)
            x = _conv_ffn(x, *ffn, wts["final_norm"], tm=tm, seq=seq, final=(l == depth - 1))
    return x.reshape(batch, seq, D_MODEL)


def kernel(x_prompt, x_sample, norm_mix, w_in_ab, sgu_gain, w_s, b_s, w_out_ab, w_in_c, conv_w_c,
           conv_b_c, w_out_c, norm_ffn, w_up, ffn_conv_w, ffn_conv_b, w_down, final_norm):
    n_even = w_in_ab.shape[0]
    wts = dict(
        norm_mix=norm_mix, norm_ffn=norm_ffn, final_norm=final_norm[None, :],
        w_in_ab=w_in_ab.astype(BF16), w_out_ab=w_out_ab.astype(BF16),
        w_s=w_s.astype(BF16), sgu_gain=sgu_gain.reshape(n_even, 1, D_A),
        bs_tile=jnp.repeat(jnp.swapaxes(b_s, 1, 2), A_HEAD_DIM, axis=2),
        cs=jnp.asarray(_channel_dft()).astype(BF16),
        w_in_c=w_in_c.astype(BF16), w_out_c=w_out_c.astype(BF16),
        conv_w_c=conv_w_c, conv_b_c=conv_b_c[:, None, :],
        w_up=w_up.astype(BF16), w_down=w_down.astype(BF16),
        ffn_conv_w=ffn_conv_w, ffn_conv_b=ffn_conv_b[:, None, :],
    )
    return _trunk(x_prompt, wts), _trunk(x_sample, wts)
```

```python
import functools

import numpy as np
import jax
import jax.numpy as jnp
from jax import lax
from jax.experimental import pallas as pl
from jax.experimental.pallas import tpu as pltpu

D_MODEL = 1024
CHUNK = 128
D_A = D_MODEL // 2
A_HEAD_DIM = 128
A_HEADS = D_A // A_HEAD_DIM
D_B = D_MODEL // 2
B_GROUP_DIM = 128
B_GROUPS = D_B // B_GROUP_DIM
D_C = D_MODEL
D_FF = ((8 * D_MODEL // 3 + 127) // 128) * 128
EPS = 1e-6

F32 = jnp.float32
BF16 = jnp.bfloat16

V7X_VMEM_BYTES = 64 * 1024 * 1024
V7X_MXU_DIM = 256
LANES = 128
F32_SUBLANES = 8
BF16_SUBLANES = 16

HALO = F32_SUBLANES
MIX_HALO = BF16_SUBLANES
TOKEN_TILE = 512
FF_COLS = V7X_MXU_DIM
EDGE_SPLIT = 2
DFT_N2 = 128
DFT_BLOCK_BYTES = 4 * 1024 * 1024
SLAB_PAD = 4
DFT_UNROLL1 = 8
DFT_UNROLL2 = 16
VMEM_LIMIT = V7X_VMEM_BYTES - 8 * 1024 * 1024


def _params(n_axes=1):
    return pltpu.CompilerParams(
        dimension_semantics=("parallel",) * n_axes, vmem_limit_bytes=VMEM_LIMIT)


def _resident(shape):
    zeros = (0,) * len(shape)
    return pl.BlockSpec(shape, lambda *_: zeros, pipeline_mode=pl.Buffered(1))


def _rmsnorm(x, g):
    return x * lax.rsqrt(jnp.mean(x * x, axis=-1, keepdims=True) + EPS) * g


def _gelu_tanh(x):
    return x * (0.5 * (1.0 + jnp.tanh(0.7978845608028654 * (x + 0.044715 * (x * x * x)))))


def _silu(x):
    hx = 0.5 * x
    return hx + hx * jnp.tanh(hx)


def _dot(a, b):
    return jnp.dot(a, b, preferred_element_type=F32)


@functools.lru_cache(maxsize=None)
def _channel_dft():
    n = B_GROUP_DIM
    k = (np.arange(n)[:, None] * np.arange(n)[None, :]) % n
    ang = 2.0 * np.pi * k / n
    m = np.concatenate([np.cos(ang), np.sin(ang)], axis=1) / np.sqrt(n)
    return m.astype(np.float32)


@functools.lru_cache(maxsize=None)
def _stage1_dft(n1):
    k = (np.arange(n1)[:, None] * np.arange(n1)[None, :]) % n1
    ang = 2.0 * np.pi * k / n1
    c, s = np.cos(ang), np.sin(ang)
    m = np.block([[c, -s], [-s, -c]]) / np.sqrt(n1)
    return m.astype(np.float32)


@functools.lru_cache(maxsize=None)
def _stage2_dft(n1, n2):
    s = n1 * n2
    k1 = np.arange(n1)[:, None, None]
    k2 = np.arange(n2)[None, :, None]
    s2 = np.arange(n2)[None, None, :]
    ang = 2.0 * np.pi * ((s2 * (k1 + n1 * k2)) % s) / s
    m = np.concatenate([np.cos(ang), np.sin(ang)], axis=2) / np.sqrt(n2)
    return m.astype(np.float32)


def _mixer_ab_front_kernel(x_ref, g_ref, win_ref, gain_ref, ws_ref, bs_ref,
                           a_ref, f_ref, *, tm):
    h = _rmsnorm(x_ref[...], g_ref[...]).astype(BF16)
    proj = _dot(h, win_ref[...])
    for hd in range(A_HEADS):
        lanes = slice(hd * A_HEAD_DIM, (hd + 1) * A_HEAD_DIM)
        u = _gelu_tanh(proj[:, hd * A_HEAD_DIM:(hd + 1) * A_HEAD_DIM])
        v = _gelu_tanh(proj[:, D_A + hd * A_HEAD_DIM:D_A + (hd + 1) * A_HEAD_DIM])
        vn = _rmsnorm(v, gain_ref[:, lanes]).astype(BF16)
        for n in range(tm // CHUNK):
            rows = slice(n * CHUNK, (n + 1) * CHUNK)
            s = _dot(ws_ref[hd], vn[rows, :]) + bs_ref[:, lanes]
            a_ref[rows, lanes] = (u[rows, :] * s).astype(BF16)
    f_ref[...] = proj[:, 2 * D_A:].astype(BF16)


def _mixer_ab_front(x, g, w_in, gain, w_s, bs_tile, *, tm):
    t = x.shape[0]
    row_f32 = pl.BlockSpec((tm, D_MODEL), lambda i: (i, 0))
    row_half = pl.BlockSpec((tm, D_A), lambda i: (i, 0))
    out = jax.ShapeDtypeStruct((t, D_A), BF16)
    return pl.pallas_call(
        functools.partial(_mixer_ab_front_kernel, tm=tm),
        out_shape=(out, out),
        grid=(t // tm,),
        in_specs=[row_f32, _resident(g.shape), _resident(w_in.shape), _resident(gain.shape),
                  _resident(w_s.shape), _resident(bs_tile.shape)],
        out_specs=(row_half, row_half),
        compiler_params=_params(),
        name="mixer_ab_front",
    )(x, g, w_in, gain, w_s, bs_tile)


def _fourier_kernel(f_ref, cs_ref, m1_ref, g2_ref, o_ref, a_scr, tr_scr, ti_scr,
                    *, n1, n2, groups, jb):
    pf = n2 + SLAB_PAD
    pt = n1 + SLAB_PAD
    for grp in range(groups):
        lanes = slice(grp * B_GROUP_DIM, (grp + 1) * B_GROUP_DIM)

        def copy_in(s1, carry):
            rows = pl.ds(pl.multiple_of(s1 * n2, n2), n2)
            a_scr[pl.ds(s1 * pf, n2), :] = f_ref[rows, lanes].astype(F32)
            return carry

        def stage1(it, carry):
            j0 = it * jb
            x = jnp.concatenate([a_scr[pl.ds(j0 + t, n1, stride=pf), :] for t in range(jb)], axis=0)
            pq = _dot(x.astype(BF16), cs_ref[...]).astype(BF16)
            p = jnp.concatenate([pq[t * n1:(t + 1) * n1, :B_GROUP_DIM] for t in range(jb)], axis=1)
            q = jnp.concatenate([pq[t * n1:(t + 1) * n1, B_GROUP_DIM:] for t in range(jb)], axis=1)
            t_all = _dot(m1_ref[...], jnp.concatenate([p, q], axis=0))
            for t in range(jb):
                cols = slice(t * B_GROUP_DIM, (t + 1) * B_GROUP_DIM)
                tr_scr[pl.ds((j0 + t) * pt, n1), :] = t_all[:n1, cols]
                ti_scr[pl.ds((j0 + t) * pt, n1), :] = t_all[n1:, cols]
            return carry

        def stage2(k1, carry):
            tr = tr_scr[pl.ds(k1, n2, stride=pt), :]
            ti = ti_scr[pl.ds(k1, n2, stride=pt), :]
            rhs = jnp.concatenate([tr, ti], axis=0).astype(BF16)
            a_scr[pl.ds(k1, n2, stride=pt), :] = _dot(g2_ref[k1], rhs)
            return carry

        def copy_out(k2, carry):
            rows = pl.ds(pl.multiple_of(k2 * n1, n1), n1)
            o_ref[rows, lanes] = a_scr[pl.ds(k2 * pt, n1), :].astype(BF16)
            return carry

        lax.fori_loop(0, n1, copy_in, 0, unroll=min(n1, 8))
        lax.fori_loop(0, n2 // jb, stage1, 0, unroll=min(n2 // jb, DFT_UNROLL1))
        lax.fori_loop(0, n1, stage2, 0, unroll=min(n1, DFT_UNROLL2))
        lax.fori_loop(0, n2, copy_out, 0, unroll=8)


def _fourier_mix(f, cs, batch, seq):
    n2 = DFT_N2
    n1 = seq // n2
    jb = V7X_MXU_DIM // n1
    assert n1 * n2 == seq and n1 % BF16_SUBLANES == 0 and V7X_MXU_DIM % n1 == 0 and n2 % jb == 0
    groups = B_GROUPS if seq * D_B * 2 <= DFT_BLOCK_BYTES else 1
    m1 = jnp.asarray(_stage1_dft(n1)).astype(BF16)
    g2 = jnp.asarray(_stage2_dft(n1, n2)).astype(BF16)
    blk = pl.BlockSpec((seq, groups * B_GROUP_DIM), lambda b, g: (b, g))
    slab_rows = max(n1 * (n2 + SLAB_PAD), n2 * (n1 + SLAB_PAD))
    return pl.pallas_call(
        functools.partial(_fourier_kernel, n1=n1, n2=n2, groups=groups, jb=jb),
        out_shape=jax.ShapeDtypeStruct((batch * seq, D_B), BF16),
        grid=(batch, B_GROUPS // groups),
        in_specs=[blk, _resident(cs.shape), _resident(m1.shape), _resident(g2.shape)],
        out_specs=blk,
        scratch_shapes=[pltpu.VMEM((slab_rows, B_GROUP_DIM), F32),
                        pltpu.VMEM((n2 * (n1 + SLAB_PAD), B_GROUP_DIM), F32),
                        pltpu.VMEM((n2 * (n1 + SLAB_PAD), B_GROUP_DIM), F32)],
        compiler_params=_params(2),
        name="fourier_mix",
    )(f, cs, m1, g2)


def _halo_specs(tm, n_tiles, halo, width):
    per = tm // halo
    last = n_tiles * per - 1
    prev = pl.BlockSpec((halo, width), lambda i: (jnp.maximum(i * per - 1, 0), 0))
    main = pl.BlockSpec((tm, width), lambda i: (i, 0))
    nxt = pl.BlockSpec((halo, width), lambda i: (jnp.minimum((i + 1) * per, last), 0))
    return prev, main, nxt


def _store_normed_with_halo(h_scr, before, tile, after, g, *, seq_tiles):
    pos = lax.rem(pl.program_id(0), seq_tiles)
    hp = jnp.where(pos != 0, _rmsnorm(before, g), 0.0)
    hn = jnp.where(pos != seq_tiles - 1, _rmsnorm(after, g), 0.0)
    h_scr[...] = jnp.concatenate([hp, _rmsnorm(tile, g), hn], axis=0).astype(BF16)


def _conv_scratch(rows):
    return pltpu.VMEM((2, FF_COLS // LANES, rows, LANES), F32)


def _store_slabs(scr, slot, row0, value):
    for s in range(FF_COLS // LANES):
        scr[slot, s, row0:row0 + value.shape[0], :] = value[:, s * LANES:(s + 1) * LANES]


def _dwconv3_rows(scr, slot, w, b, row0, n):
    out = []
    for s in range(FF_COLS // LANES):
        ws = w[:, s * LANES:(s + 1) * LANES]
        out.append(scr[slot, s, pl.ds(HALO + row0 - 1, n), :] * ws[0:1, :]
                   + scr[slot, s, pl.ds(HALO + row0, n), :] * ws[1:2, :]
                   + scr[slot, s, pl.ds(HALO + row0 + 1, n), :] * ws[2:3, :]
                   + b[:, s * LANES:(s + 1) * LANES])
    return jnp.concatenate(out, axis=-1)


def _row_chunks(n, parts, align):
    cuts = [min(n, -(-(n * p // parts) // align) * align) for p in range(parts + 1)]
    return [(lo, hi) for lo, hi in zip(cuts[:-1], cuts[1:]) if hi > lo]


def _ffn_pipeline(h_scr, pg_scr, pv_scr, wup_ref, cw_ref, cb_ref, wdn_ref, o_ref, *, tm):
    n_blocks = D_FF // FF_COLS

    def gate_cols(c):
        return slice(c * FF_COLS, (c + 1) * FF_COLS)

    def val_cols(c):
        return slice(D_FF + c * FF_COLS, D_FF + (c + 1) * FF_COLS)

    def up_project(c, parts):
        for lo, hi in _row_chunks(tm + 2 * HALO, parts, BF16_SUBLANES):
            h = h_scr[lo:hi, :]
            _store_slabs(pg_scr, c % 2, lo, _dot(h, wup_ref[:, gate_cols(c)]))
            _store_slabs(pv_scr, c % 2, lo, _dot(h, wup_ref[:, val_cols(c)]))

    def conv_act_down(c, parts):
        for lo, hi in _row_chunks(tm, parts, BF16_SUBLANES):
            gate = _dwconv3_rows(pg_scr, c % 2, cw_ref[:, gate_cols(c)], cb_ref[:, gate_cols(c)],
                                 lo, hi - lo)
            val = _dwconv3_rows(pv_scr, c % 2, cw_ref[:, val_cols(c)], cb_ref[:, val_cols(c)],
                                lo, hi - lo)
            act = (_silu(gate) * val).astype(BF16)
            o_ref[lo:hi, :] += _dot(act, wdn_ref[gate_cols(c), :])

    up_project(0, EDGE_SPLIT)
    for c in range(n_blocks):
        if c + 1 < n_blocks:
            up_project(c + 1, 1)
        conv_act_down(c, EDGE_SPLIT if c == n_blocks - 1 else 1)


def _conv_ffn_kernel(xp_ref, x_ref, xn_ref, g_ref, wup_ref, cw_ref, cb_ref, wdn_ref, fin_ref,
                     o_ref, h_scr, pg_scr, pv_scr, *, tm, seq_tiles, final):
    _store_normed_with_halo(h_scr, xp_ref[...], x_ref[...], xn_ref[...], g_ref[...],
                            seq_tiles=seq_tiles)
    o_ref[...] = x_ref[...]
    _ffn_pipeline(h_scr, pg_scr, pv_scr, wup_ref, cw_ref, cb_ref, wdn_ref, o_ref, tm=tm)
    if final:
        o_ref[...] = _rmsnorm(o_ref[...], fin_ref[...])


def _mix_out_conv_ffn_kernel(xp_ref, x_ref, xn_ref, ap_ref, a_ref, an_ref, fp_ref, f_ref, fn_ref,
                             wmix_ref, g_ref, wup_ref, cw_ref, cb_ref, wdn_ref,
                             o_ref, h_scr, pg_scr, pv_scr, *, tm, seq_tiles):
    mix = jnp.concatenate(
        [jnp.concatenate([ap_ref[...], a_ref[...], an_ref[...]], axis=0),
         jnp.concatenate([fp_ref[...], f_ref[...], fn_ref[...]], axis=0)], axis=1)
    x_all = (jnp.concatenate([xp_ref[...], x_ref[...], xn_ref[...]], axis=0)
             + _dot(mix, wmix_ref[...]))
    lo = MIX_HALO - HALO
    _store_normed_with_halo(h_scr, x_all[lo:MIX_HALO], x_all[MIX_HALO:MIX_HALO + tm],
                            x_all[MIX_HALO + tm:MIX_HALO + tm + HALO], g_ref[...],
                            seq_tiles=seq_tiles)
    o_ref[...] = x_all[MIX_HALO:MIX_HALO + tm]
    _ffn_pipeline(h_scr, pg_scr, pv_scr, wup_ref, cw_ref, cb_ref, wdn_ref, o_ref, tm=tm)


def _ffn_scratch(tm):
    rows = tm + 2 * HALO
    return [pltpu.VMEM((rows, D_MODEL), BF16), _conv_scratch(rows), _conv_scratch(rows)]


def _conv_ffn(x, g, w_up, conv_w, conv_b, w_down, fin, *, tm, seq, final):
    t = x.shape[0]
    n_tiles = t // tm
    prev, main, nxt = _halo_specs(tm, n_tiles, HALO, D_MODEL)
    return pl.pallas_call(
        functools.partial(_conv_ffn_kernel, tm=tm, seq_tiles=seq // tm, final=final),
        out_shape=jax.ShapeDtypeStruct((t, D_MODEL), F32),
        grid=(n_tiles,),
        in_specs=[prev, main, nxt, _resident(g.shape), _resident(w_up.shape),
                  _resident(conv_w.shape), _resident(conv_b.shape), _resident(w_down.shape),
                  _resident(fin.shape)],
        out_specs=main,
        scratch_shapes=_ffn_scratch(tm),
        compiler_params=_params(),
        name="conv_ffn",
    )(x, x, x, g, w_up, conv_w, conv_b, w_down, fin)


def _mix_out_conv_ffn(x, a, ff, w_mix, g, w_up, conv_w, conv_b, w_down, *, tm, seq):
    t = x.shape[0]
    n_tiles = t // tm
    x_specs = _halo_specs(tm, n_tiles, MIX_HALO, D_MODEL)
    half_specs = _halo_specs(tm, n_tiles, MIX_HALO, D_A)
    return pl.pallas_call(
        functools.partial(_mix_out_conv_ffn_kernel, tm=tm, seq_tiles=seq // tm),
        out_shape=jax.ShapeDtypeStruct((t, D_MODEL), F32),
        grid=(n_tiles,),
        in_specs=[*x_specs, *half_specs, *half_specs, _resident(w_mix.shape), _resident(g.shape),
                  _resident(w_up.shape), _resident(conv_w.shape), _resident(conv_b.shape),
                  _resident(w_down.shape)],
        out_specs=x_specs[1],
        scratch_shapes=_ffn_scratch(tm),
        compiler_params=_params(),
        name="mix_out_conv_ffn",
    )(x, x, x, a, a, a, ff, ff, ff, w_mix, g, w_up, conv_w, conv_b, w_down)


def _mixer_c_kernel(xp_ref, x_ref, xn_ref, g_ref, win_ref, cw_ref, cb_ref, wout_ref,
                    o_ref, h_scr, cz_scr, y_scr, *, tm, seq_tiles):
    _store_normed_with_halo(h_scr, xp_ref[...], x_ref[...], xn_ref[...], g_ref[...],
                            seq_tiles=seq_tiles)
    h = h_scr[...]
    for c in range(D_C // FF_COLS):
        slot = c % 2
        cols = slice(c * FF_COLS, (c + 1) * FF_COLS)
        gate_b = _dot(h, _proj_cols(win_ref, 0, c))[HALO:HALO + tm, :]
        gate_c = _dot(h, _proj_cols(win_ref, 1, c))
        z = _dot(h, _proj_cols(win_ref, 2, c))
        _store_slabs(cz_scr, slot, 0, gate_c * z)
        conv = _dwconv3_rows(cz_scr, slot, cw_ref[:, cols], cb_ref[:, cols], 0, tm)
        y_scr[:, cols] = (gate_b * conv).astype(BF16)
    o_ref[...] = x_ref[...] + _dot(y_scr[...], wout_ref[...])


def _proj_cols(w_ref, part, c):
    start = part * D_C + c * FF_COLS
    return w_ref[:, start:start + FF_COLS]


def _mixer_c(x, g, w_in, conv_w, conv_b, w_out, *, tm, seq):
    t = x.shape[0]
    n_tiles = t // tm
    prev, main, nxt = _halo_specs(tm, n_tiles, HALO, D_MODEL)
    rows = tm + 2 * HALO
    return pl.pallas_call(
        functools.partial(_mixer_c_kernel, tm=tm, seq_tiles=seq // tm),
        out_shape=jax.ShapeDtypeStruct((t, D_MODEL), F32),
        grid=(n_tiles,),
        in_specs=[prev, main, nxt, _resident(g.shape), _resident(w_in.shape),
                  _resident(conv_w.shape), _resident(conv_b.shape), _resident(w_out.shape)],
        out_specs=main,
        scratch_shapes=[pltpu.VMEM((rows, D_MODEL), BF16),
                        _conv_scratch(rows),
                        pltpu.VMEM((tm, D_C), BF16)],
        compiler_params=_params(),
        name="mixer_c",
    )(x, x, x, g, w_in, conv_w, conv_b, w_out)


def _trunk(x3, wts):
    batch, seq, _ = x3.shape
    tm = min(TOKEN_TILE, seq)
    assert seq % tm == 0 and tm % CHUNK == 0 and seq % DFT_N2 == 0
    x = x3.reshape(batch * seq, D_MODEL)
    depth = wts["norm_mix"].shape[0]
    assert depth % 2 == 0
    for l in range(depth):
        i = l // 2
        g_mix = wts["norm_mix"][l][None, :]
        ffn = (wts["norm_ffn"][l][None, :], wts["w_up"][l], wts["ffn_conv_w"][l],
               wts["ffn_conv_b"][l], wts["w_down"][l])
        if l % 2 == 0:
            a, f = _mixer_ab_front(x, g_mix, wts["w_in_ab"][i], wts["sgu_gain"][i],
                                   wts["w_s"][i], wts["bs_tile"][i], tm=tm)
            ff = _fourier_mix(f, wts["cs"], batch, seq)
            x = _mix_out_conv_ffn(x, a, ff, wts["w_out_ab"][i], *ffn, tm=tm, seq=seq)
        else:
            x = _mixer_c(x, g_mix, wts["w_in_c"][i], wts["conv_w_c"][i], wts["conv_b_c"][i],
                         wts["w_out_c"][i], tm=tm, seq=seq)
            x = _conv_ffn(x, *ffn, wts["final_norm"], tm=tm, seq=seq, final=(l == depth - 1))
    return x.reshape(batch, seq, D_MODEL)


def kernel(x_prompt, x_sample, norm_mix, w_in_ab, sgu_gain, w_s, b_s, w_out_ab, w_in_c, conv_w_c,
           conv_b_c, w_out_c, norm_ffn, w_up, ffn_conv_w, ffn_conv_b, w_down, final_norm):
    n_even = w_in_ab.shape[0]

    def per_layer_bf16(w):
        return [w[l].astype(BF16) for l in range(w.shape[0])]

    wts = dict(
        norm_mix=norm_mix, norm_ffn=norm_ffn, final_norm=final_norm[None, :],
        w_in_ab=per_layer_bf16(w_in_ab), w_out_ab=per_layer_bf16(w_out_ab),
        w_s=per_layer_bf16(w_s), sgu_gain=sgu_gain.reshape(n_even, 1, D_A),
        bs_tile=jnp.repeat(jnp.swapaxes(b_s, 1, 2), A_HEAD_DIM, axis=2),
        cs=jnp.asarray(_channel_dft()).astype(BF16),
        w_in_c=per_layer_bf16(w_in_c), w_out_c=per_layer_bf16(w_out_c),
        conv_w_c=conv_w_c, conv_b_c=conv_b_c[:, None, :],
        w_up=per_layer_bf16(w_up), w_down=per_layer_bf16(w_down),
        ffn_conv_w=ffn_conv_w, ffn_conv_b=ffn_conv_b[:, None, :],
    )
    return _trunk(x_prompt, wts), _trunk(x_sample, wts)
```

```python
import functools

import numpy as np
import jax
import jax.numpy as jnp
from jax import lax
from jax.experimental import pallas as pl
from jax.experimental.pallas import tpu as pltpu

D_MODEL = 1024
CHUNK = 128
D_A = D_MODEL // 2
A_HEAD_DIM = 128
A_HEADS = D_A // A_HEAD_DIM
D_B = D_MODEL // 2
B_GROUP_DIM = 128
B_GROUPS = D_B // B_GROUP_DIM
D_C = D_MODEL
D_FF = ((8 * D_MODEL // 3 + 127) // 128) * 128
EPS = 1e-6

F32 = jnp.float32
BF16 = jnp.bfloat16

V7X_VMEM_BYTES = 64 * 1024 * 1024
V7X_MXU_DIM = 256
LANES = 128
F32_SUBLANES = 8
BF16_SUBLANES = 16

HALO = F32_SUBLANES
MIX_HALO = BF16_SUBLANES
TOKEN_TILE = 512
FF_COLS = V7X_MXU_DIM
EDGE_SPLIT = 2
UP_AHEAD = 2
CONV_SLOTS = UP_AHEAD + 1
DFT_N2 = 128
DFT_BLOCK_BYTES = 4 * 1024 * 1024
SLAB_PAD = 4
DFT_UNROLL1 = 8
DFT_UNROLL2 = 16
VMEM_LIMIT = V7X_VMEM_BYTES - 8 * 1024 * 1024


def _params(n_axes=1):
    return pltpu.CompilerParams(
        dimension_semantics=("parallel",) * n_axes, vmem_limit_bytes=VMEM_LIMIT)


def _resident(shape):
    zeros = (0,) * len(shape)
    return pl.BlockSpec(shape, lambda *_: zeros, pipeline_mode=pl.Buffered(1))


def _resident_layer(stack_shape, layer):
    index = (layer,) + (0,) * (len(stack_shape) - 1)
    return pl.BlockSpec((None,) + tuple(stack_shape[1:]), lambda *_: index,
                        pipeline_mode=pl.Buffered(1))


def _rmsnorm(x, g):
    return x * lax.rsqrt(jnp.mean(x * x, axis=-1, keepdims=True) + EPS) * g


def _gelu_tanh(x):
    return x * (0.5 * (1.0 + jnp.tanh(0.7978845608028654 * (x + 0.044715 * (x * x * x)))))


def _silu(x):
    hx = 0.5 * x
    return hx + hx * jnp.tanh(hx)


def _dot(a, b):
    return jnp.dot(a, b, preferred_element_type=F32)


@functools.lru_cache(maxsize=None)
def _channel_dft():
    n = B_GROUP_DIM
    k = (np.arange(n)[:, None] * np.arange(n)[None, :]) % n
    ang = 2.0 * np.pi * k / n
    m = np.concatenate([np.cos(ang), np.sin(ang)], axis=1) / np.sqrt(n)
    return m.astype(np.float32)


@functools.lru_cache(maxsize=None)
def _stage1_dft(n1):
    k = (np.arange(n1)[:, None] * np.arange(n1)[None, :]) % n1
    ang = 2.0 * np.pi * k / n1
    c, s = np.cos(ang), np.sin(ang)
    m = np.block([[c, -s], [-s, -c]]) / np.sqrt(n1)
    return m.astype(np.float32)


@functools.lru_cache(maxsize=None)
def _stage2_dft(n1, n2):
    s = n1 * n2
    k1 = np.arange(n1)[:, None, None]
    k2 = np.arange(n2)[None, :, None]
    s2 = np.arange(n2)[None, None, :]
    ang = 2.0 * np.pi * ((s2 * (k1 + n1 * k2)) % s) / s
    m = np.concatenate([np.cos(ang), np.sin(ang)], axis=2) / np.sqrt(n2)
    return m.astype(np.float32)


def _mixer_ab_front_kernel(x_ref, g_ref, win_ref, gain_ref, ws_ref, bs_ref,
                           a_ref, f_ref, *, tm):
    h = _rmsnorm(x_ref[...], g_ref[...]).astype(BF16)
    proj = _dot(h, win_ref[...])
    for hd in range(A_HEADS):
        lanes = slice(hd * A_HEAD_DIM, (hd + 1) * A_HEAD_DIM)
        u = _gelu_tanh(proj[:, hd * A_HEAD_DIM:(hd + 1) * A_HEAD_DIM])
        v = _gelu_tanh(proj[:, D_A + hd * A_HEAD_DIM:D_A + (hd + 1) * A_HEAD_DIM])
        vn = _rmsnorm(v, gain_ref[:, lanes]).astype(BF16)
        chunks = [slice(n * CHUNK, (n + 1) * CHUNK) for n in range(tm // CHUNK)]
        s_all = _dot(ws_ref[hd], jnp.concatenate([vn[rows, :] for rows in chunks], axis=1))
        for n, rows in enumerate(chunks):
            s = s_all[:, n * A_HEAD_DIM:(n + 1) * A_HEAD_DIM] + bs_ref[:, lanes]
            a_ref[rows, lanes] = (u[rows, :] * s).astype(BF16)
    f_ref[...] = proj[:, 2 * D_A:].astype(BF16)


def _mixer_ab_front(x, g, w_in, gain, w_s, bs_tile, *, tm):
    t = x.shape[0]
    row_f32 = pl.BlockSpec((tm, D_MODEL), lambda i: (i, 0))
    row_half = pl.BlockSpec((tm, D_A), lambda i: (i, 0))
    out = jax.ShapeDtypeStruct((t, D_A), BF16)
    return pl.pallas_call(
        functools.partial(_mixer_ab_front_kernel, tm=tm),
        out_shape=(out, out),
        grid=(t // tm,),
        in_specs=[row_f32, _resident(g.shape), _resident(w_in.shape), _resident(gain.shape),
                  _resident(w_s.shape), _resident(bs_tile.shape)],
        out_specs=(row_half, row_half),
        compiler_params=_params(),
        name="mixer_ab_front",
    )(x, g, w_in, gain, w_s, bs_tile)


def _fourier_kernel(f_ref, cs_ref, m1_ref, g2_ref, o_ref, a_scr, tr_scr, ti_scr,
                    *, n1, n2, groups, jb):
    pf = n2 + SLAB_PAD
    pt = n1 + SLAB_PAD
    for grp in range(groups):
        lanes = slice(grp * B_GROUP_DIM, (grp + 1) * B_GROUP_DIM)

        def copy_in(s1, carry):
            rows = pl.ds(pl.multiple_of(s1 * n2, n2), n2)
            a_scr[pl.ds(s1 * pf, n2), :] = f_ref[rows, lanes].astype(F32)
            return carry

        def stage1(it, carry):
            j0 = it * jb
            x = jnp.concatenate([a_scr[pl.ds(j0 + t, n1, stride=pf), :] for t in range(jb)], axis=0)
            pq = _dot(x.astype(BF16), cs_ref[...]).astype(BF16)
            p = jnp.concatenate([pq[t * n1:(t + 1) * n1, :B_GROUP_DIM] for t in range(jb)], axis=1)
            q = jnp.concatenate([pq[t * n1:(t + 1) * n1, B_GROUP_DIM:] for t in range(jb)], axis=1)
            t_all = _dot(m1_ref[...], jnp.concatenate([p, q], axis=0))
            for t in range(jb):
                cols = slice(t * B_GROUP_DIM, (t + 1) * B_GROUP_DIM)
                tr_scr[pl.ds((j0 + t) * pt, n1), :] = t_all[:n1, cols]
                ti_scr[pl.ds((j0 + t) * pt, n1), :] = t_all[n1:, cols]
            return carry

        def stage2(k1, carry):
            tr = tr_scr[pl.ds(k1, n2, stride=pt), :]
            ti = ti_scr[pl.ds(k1, n2, stride=pt), :]
            rhs = jnp.concatenate([tr, ti], axis=0).astype(BF16)
            a_scr[pl.ds(k1, n2, stride=pt), :] = _dot(g2_ref[k1], rhs)
            return carry

        def copy_out(k2, carry):
            rows = pl.ds(pl.multiple_of(k2 * n1, n1), n1)
            o_ref[rows, lanes] = a_scr[pl.ds(k2 * pt, n1), :].astype(BF16)
            return carry

        lax.fori_loop(0, n1, copy_in, 0, unroll=min(n1, 8))
        lax.fori_loop(0, n2 // jb, stage1, 0, unroll=min(n2 // jb, DFT_UNROLL1))
        lax.fori_loop(0, n1, stage2, 0, unroll=min(n1, DFT_UNROLL2))
        lax.fori_loop(0, n2, copy_out, 0, unroll=8)


def _fourier_mix(f, cs, batch, seq):
    n2 = DFT_N2
    n1 = seq // n2
    jb = V7X_MXU_DIM // n1
    assert n1 * n2 == seq and n1 % BF16_SUBLANES == 0 and V7X_MXU_DIM % n1 == 0 and n2 % jb == 0
    groups = B_GROUPS if seq * D_B * 2 <= DFT_BLOCK_BYTES else 1
    m1 = jnp.asarray(_stage1_dft(n1)).astype(BF16)
    g2 = jnp.asarray(_stage2_dft(n1, n2)).astype(BF16)
    blk = pl.BlockSpec((seq, groups * B_GROUP_DIM), lambda b, g: (b, g))
    slab_rows = max(n1 * (n2 + SLAB_PAD), n2 * (n1 + SLAB_PAD))
    return pl.pallas_call(
        functools.partial(_fourier_kernel, n1=n1, n2=n2, groups=groups, jb=jb),
        out_shape=jax.ShapeDtypeStruct((batch * seq, D_B), BF16),
        grid=(batch, B_GROUPS // groups),
        in_specs=[blk, _resident(cs.shape), _resident(m1.shape), _resident(g2.shape)],
        out_specs=blk,
        scratch_shapes=[pltpu.VMEM((slab_rows, B_GROUP_DIM), F32),
                        pltpu.VMEM((n2 * (n1 + SLAB_PAD), B_GROUP_DIM), F32),
                        pltpu.VMEM((n2 * (n1 + SLAB_PAD), B_GROUP_DIM), F32)],
        compiler_params=_params(2),
        name="fourier_mix",
    )(f, cs, m1, g2)


def _halo_specs(tm, n_tiles, halo, width):
    per = tm // halo
    last = n_tiles * per - 1
    prev = pl.BlockSpec((halo, width), lambda i: (jnp.maximum(i * per - 1, 0), 0))
    main = pl.BlockSpec((tm, width), lambda i: (i, 0))
    nxt = pl.BlockSpec((halo, width), lambda i: (jnp.minimum((i + 1) * per, last), 0))
    return prev, main, nxt


def _store_normed_with_halo(h_scr, before, tile, after, g, *, seq_tiles):
    pos = lax.rem(pl.program_id(0), seq_tiles)
    hp = jnp.where(pos != 0, _rmsnorm(before, g), 0.0)
    hn = jnp.where(pos != seq_tiles - 1, _rmsnorm(after, g), 0.0)
    h_scr[...] = jnp.concatenate([hp, _rmsnorm(tile, g), hn], axis=0).astype(BF16)


def _conv_scratch(rows):
    return pltpu.VMEM((CONV_SLOTS, FF_COLS // LANES, rows, LANES), F32)


def _store_slabs(scr, slot, row0, value):
    for s in range(FF_COLS // LANES):
        scr[slot, s, row0:row0 + value.shape[0], :] = value[:, s * LANES:(s + 1) * LANES]


def _dwconv3_rows(scr, slot, w, b, row0, n):
    out = []
    for s in range(FF_COLS // LANES):
        ws = w[:, s * LANES:(s + 1) * LANES]
        out.append(scr[slot, s, pl.ds(HALO + row0 - 1, n), :] * ws[0:1, :]
                   + scr[slot, s, pl.ds(HALO + row0, n), :] * ws[1:2, :]
                   + scr[slot, s, pl.ds(HALO + row0 + 1, n), :] * ws[2:3, :]
                   + b[:, s * LANES:(s + 1) * LANES])
    return jnp.concatenate(out, axis=-1)


def _row_chunks(n, parts, align):
    cuts = [min(n, -(-(n * p // parts) // align) * align) for p in range(parts + 1)]
    return [(lo, hi) for lo, hi in zip(cuts[:-1], cuts[1:]) if hi > lo]


def _ffn_pipeline(h_scr, pg_scr, pv_scr, wup_ref, cw_ref, cb_ref, wdn_ref, o_ref, *, tm,
                  finish_rows=None):
    n_blocks = D_FF // FF_COLS

    def gate_cols(c):
        return slice(c * FF_COLS, (c + 1) * FF_COLS)

    def val_cols(c):
        return slice(D_FF + c * FF_COLS, D_FF + (c + 1) * FF_COLS)

    def up_project(c, parts):
        slot = c % CONV_SLOTS
        for lo, hi in _row_chunks(tm + 2 * HALO, parts, BF16_SUBLANES):
            h = h_scr[lo:hi, :]
            _store_slabs(pg_scr, slot, lo, _dot(h, wup_ref[:, gate_cols(c)]))
            _store_slabs(pv_scr, slot, lo, _dot(h, wup_ref[:, val_cols(c)]))

    def conv_act_down(c, parts):
        slot = c % CONV_SLOTS
        for lo, hi in _row_chunks(tm, parts, BF16_SUBLANES):
            gate = _dwconv3_rows(pg_scr, slot, cw_ref[:, gate_cols(c)], cb_ref[:, gate_cols(c)],
                                 lo, hi - lo)
            val = _dwconv3_rows(pv_scr, slot, cw_ref[:, val_cols(c)], cb_ref[:, val_cols(c)],
                                lo, hi - lo)
            act = (_silu(gate) * val).astype(BF16)
            o_ref[lo:hi, :] += _dot(act, wdn_ref[gate_cols(c), :])
            if finish_rows is not None and c == n_blocks - 1:
                finish_rows(lo, hi)

    for c in range(min(UP_AHEAD, n_blocks)):
        up_project(c, EDGE_SPLIT if c == 0 else 1)
    for c in range(n_blocks):
        if c + UP_AHEAD < n_blocks:
            up_project(c + UP_AHEAD, 1)
        conv_act_down(c, EDGE_SPLIT if c == n_blocks - 1 else 1)


def _conv_ffn_kernel(xp_ref, x_ref, xn_ref, g_ref, wup_ref, cw_ref, cb_ref, wdn_ref, fin_ref,
                     o_ref, h_scr, pg_scr, pv_scr, *, tm, seq_tiles, final):
    _store_normed_with_halo(h_scr, xp_ref[...], x_ref[...], xn_ref[...], g_ref[...],
                            seq_tiles=seq_tiles)
    o_ref[...] = x_ref[...]

    def final_norm(lo, hi):
        o_ref[lo:hi, :] = _rmsnorm(o_ref[lo:hi, :], fin_ref[...])

    _ffn_pipeline(h_scr, pg_scr, pv_scr, wup_ref, cw_ref, cb_ref, wdn_ref, o_ref, tm=tm,
                  finish_rows=final_norm if final else None)


def _mix_out_conv_ffn_kernel(xp_ref, x_ref, xn_ref, ap_ref, a_ref, an_ref, fp_ref, f_ref, fn_ref,
                             wmix_ref, g_ref, wup_ref, cw_ref, cb_ref, wdn_ref,
                             o_ref, h_scr, pg_scr, pv_scr, *, tm, seq_tiles):
    mix = jnp.concatenate(
        [jnp.concatenate([ap_ref[...], a_ref[...], an_ref[...]], axis=0),
         jnp.concatenate([fp_ref[...], f_ref[...], fn_ref[...]], axis=0)], axis=1)
    x_all = (jnp.concatenate([xp_ref[...], x_ref[...], xn_ref[...]], axis=0)
             + _dot(mix, wmix_ref[...]))
    lo = MIX_HALO - HALO
    _store_normed_with_halo(h_scr, x_all[lo:MIX_HALO], x_all[MIX_HALO:MIX_HALO + tm],
                            x_all[MIX_HALO + tm:MIX_HALO + tm + HALO], g_ref[...],
                            seq_tiles=seq_tiles)
    o_ref[...] = x_all[MIX_HALO:MIX_HALO + tm]
    _ffn_pipeline(h_scr, pg_scr, pv_scr, wup_ref, cw_ref, cb_ref, wdn_ref, o_ref, tm=tm)


def _ffn_scratch(tm):
    rows = tm + 2 * HALO
    return [pltpu.VMEM((rows, D_MODEL), BF16), _conv_scratch(rows), _conv_scratch(rows)]


def _conv_ffn(x, g, w_up, conv_w, conv_b, w_down, fin, *, layer, tm, seq, final):
    t = x.shape[0]
    n_tiles = t // tm
    prev, main, nxt = _halo_specs(tm, n_tiles, HALO, D_MODEL)
    return pl.pallas_call(
        functools.partial(_conv_ffn_kernel, tm=tm, seq_tiles=seq // tm, final=final),
        out_shape=jax.ShapeDtypeStruct((t, D_MODEL), F32),
        grid=(n_tiles,),
        in_specs=[prev, main, nxt, _resident(g.shape), _resident_layer(w_up.shape, layer),
                  _resident(conv_w.shape), _resident(conv_b.shape),
                  _resident_layer(w_down.shape, layer), _resident(fin.shape)],
        out_specs=main,
        scratch_shapes=_ffn_scratch(tm),
        compiler_params=_params(),
        name="conv_ffn",
    )(x, x, x, g, w_up, conv_w, conv_b, w_down, fin)


def _mix_out_conv_ffn(x, a, ff, w_mix, g, w_up, conv_w, conv_b, w_down, *, layer, tm, seq):
    t = x.shape[0]
    n_tiles = t // tm
    x_specs = _halo_specs(tm, n_tiles, MIX_HALO, D_MODEL)
    half_specs = _halo_specs(tm, n_tiles, MIX_HALO, D_A)
    return pl.pallas_call(
        functools.partial(_mix_out_conv_ffn_kernel, tm=tm, seq_tiles=seq // tm),
        out_shape=jax.ShapeDtypeStruct((t, D_MODEL), F32),
        grid=(n_tiles,),
        in_specs=[*x_specs, *half_specs, *half_specs, _resident(w_mix.shape), _resident(g.shape),
                  _resident_layer(w_up.shape, layer), _resident(conv_w.shape),
                  _resident(conv_b.shape), _resident_layer(w_down.shape, layer)],
        out_specs=x_specs[1],
        scratch_shapes=_ffn_scratch(tm),
        compiler_params=_params(),
        name="mix_out_conv_ffn",
    )(x, x, x, a, a, a, ff, ff, ff, w_mix, g, w_up, conv_w, conv_b, w_down)


def _mixer_c_kernel(xp_ref, x_ref, xn_ref, g_ref, win_ref, cw_ref, cb_ref, wout_ref,
                    o_ref, h_scr, cz_scr, y_scr, *, tm, seq_tiles):
    _store_normed_with_halo(h_scr, xp_ref[...], x_ref[...], xn_ref[...], g_ref[...],
                            seq_tiles=seq_tiles)
    h = h_scr[...]
    for c in range(D_C // FF_COLS):
        slot = c % CONV_SLOTS
        cols = slice(c * FF_COLS, (c + 1) * FF_COLS)
        gate_b = _dot(h, _proj_cols(win_ref, 0, c))[HALO:HALO + tm, :]
        gate_c = _dot(h, _proj_cols(win_ref, 1, c))
        z = _dot(h, _proj_cols(win_ref, 2, c))
        _store_slabs(cz_scr, slot, 0, gate_c * z)
        conv = _dwconv3_rows(cz_scr, slot, cw_ref[:, cols], cb_ref[:, cols], 0, tm)
        y_scr[:, cols] = (gate_b * conv).astype(BF16)
    o_ref[...] = x_ref[...] + _dot(y_scr[...], wout_ref[...])


def _proj_cols(w_ref, part, c):
    start = part * D_C + c * FF_COLS
    return w_ref[:, start:start + FF_COLS]


def _mixer_c(x, g, w_in, conv_w, conv_b, w_out, *, tm, seq):
    t = x.shape[0]
    n_tiles = t // tm
    prev, main, nxt = _halo_specs(tm, n_tiles, HALO, D_MODEL)
    rows = tm + 2 * HALO
    return pl.pallas_call(
        functools.partial(_mixer_c_kernel, tm=tm, seq_tiles=seq // tm),
        out_shape=jax.ShapeDtypeStruct((t, D_MODEL), F32),
        grid=(n_tiles,),
        in_specs=[prev, main, nxt, _resident(g.shape), _resident(w_in.shape),
                  _resident(conv_w.shape), _resident(conv_b.shape), _resident(w_out.shape)],
        out_specs=main,
        scratch_shapes=[pltpu.VMEM((rows, D_MODEL), BF16),
                        _conv_scratch(rows),
                        pltpu.VMEM((tm, D_C), BF16)],
        compiler_params=_params(),
        name="mixer_c",
    )(x, x, x, g, w_in, conv_w, conv_b, w_out)


def _trunk(x3, wts):
    batch, seq, _ = x3.shape
    tm = min(TOKEN_TILE, seq)
    assert seq % tm == 0 and tm % CHUNK == 0 and seq % DFT_N2 == 0
    x = x3.reshape(batch * seq, D_MODEL)
    depth = wts["norm_mix"].shape[0]
    assert depth % 2 == 0
    for l in range(depth):
        i = l // 2
        g_mix = wts["norm_mix"][l][None, :]
        ffn = (wts["norm_ffn"][l][None, :], wts["w_up"], wts["ffn_conv_w"][l],
               wts["ffn_conv_b"][l], wts["w_down"])
        if l % 2 == 0:
            a, f = _mixer_ab_front(x, g_mix, wts["w_in_ab"][i], wts["sgu_gain"][i],
                                   wts["w_s"][i], wts["bs_tile"][i], tm=tm)
            ff = _fourier_mix(f, wts["cs"], batch, seq)
            x = _mix_out_conv_ffn(x, a, ff, wts["w_out_ab"][i], *ffn, layer=l, tm=tm, seq=seq)
        else:
            x = _mixer_c(x, g_mix, wts["w_in_c"][i], wts["conv_w_c"][i], wts["conv_b_c"][i],
                         wts["w_out_c"][i], tm=tm, seq=seq)
            x = _conv_ffn(x, *ffn, wts["final_norm"], layer=l, tm=tm, seq=seq,
                          final=(l == depth - 1))
    return x.reshape(batch, seq, D_MODEL)


def kernel(x_prompt, x_sample, norm_mix, w_in_ab, sgu_gain, w_s, b_s, w_out_ab, w_in_c, conv_w_c,
           conv_b_c, w_out_c, norm_ffn, w_up, ffn_conv_w, ffn_conv_b, w_down, final_norm):
    n_even = w_in_ab.shape[0]
    wts = dict(
        norm_mix=norm_mix, norm_ffn=norm_ffn, final_norm=final_norm[None, :],
        w_in_ab=w_in_ab.astype(BF16), w_out_ab=w_out_ab.astype(BF16),
        w_s=w_s.astype(BF16), sgu_gain=sgu_gain.reshape(n_even, 1, D_A),
        bs_tile=jnp.repeat(jnp.swapaxes(b_s, 1, 2), A_HEAD_DIM, axis=2),
        cs=jnp.asarray(_channel_dft()).astype(BF16),
        w_in_c=w_in_c.astype(BF16), w_out_c=w_out_c.astype(BF16),
        conv_w_c=conv_w_c, conv_b_c=conv_b_c[:, None, :],
        w_up=w_up.astype(BF16), w_down=w_down.astype(BF16),
        ffn_conv_w=ffn_conv_w, ffn_conv_b=ffn_conv_b[:, None, :],
    )
    return _trunk(x_prompt, wts), _trunk(x_sample, wts)
```

```python
import functools

import numpy as np
import jax
import jax.numpy as jnp
from jax import lax
from jax.experimental import pallas as pl
from jax.experimental.pallas import tpu as pltpu

D_MODEL = 1024
CHUNK = 128
D_A = D_MODEL // 2
A_HEAD_DIM = 128
A_HEADS = D_A // A_HEAD_DIM
D_B = D_MODEL // 2
B_GROUP_DIM = 128
B_GROUPS = D_B // B_GROUP_DIM
D_C = D_MODEL
D_FF = ((8 * D_MODEL // 3 + 127) // 128) * 128
EPS = 1e-6

F32 = jnp.float32
BF16 = jnp.bfloat16

V7X_VMEM_BYTES = 64 * 1024 * 1024
V7X_MXU_DIM = 256
LANES = 128
F32_SUBLANES = 8
BF16_SUBLANES = 16

HALO = F32_SUBLANES
MIX_HALO = BF16_SUBLANES
TOKEN_TILE = 512
MIXER_TILE = 1024
FF_COLS = V7X_MXU_DIM
EDGE_SPLIT = 2
UP_AHEAD = 2
CONV_SLOTS = UP_AHEAD + 1
DOWN_GROUP = 2
DFT_N2 = 128
DFT_BLOCK_BYTES = 4 * 1024 * 1024
SLAB_PAD = 4
DFT_UNROLL1 = 8
DFT_UNROLL2 = 16
VMEM_LIMIT = V7X_VMEM_BYTES - 8 * 1024 * 1024


def _params(n_axes=1):
    return pltpu.CompilerParams(
        dimension_semantics=("parallel",) * n_axes, vmem_limit_bytes=VMEM_LIMIT)


def _resident(shape):
    zeros = (0,) * len(shape)
    return pl.BlockSpec(shape, lambda *_: zeros, pipeline_mode=pl.Buffered(1))


def _resident_layer(stack_shape, layer):
    index = (layer,) + (0,) * (len(stack_shape) - 1)
    return pl.BlockSpec((None,) + tuple(stack_shape[1:]), lambda *_: index,
                        pipeline_mode=pl.Buffered(1))


def _rmsnorm(x, g):
    return x * lax.rsqrt(jnp.mean(x * x, axis=-1, keepdims=True) + EPS) * g


def _gelu_tanh(x):
    return x * (0.5 * (1.0 + jnp.tanh(0.7978845608028654 * (x + 0.044715 * (x * x * x)))))


def _silu(x):
    hx = 0.5 * x
    return hx + hx * jnp.tanh(hx)


def _dot(a, b):
    return jnp.dot(a, b, preferred_element_type=F32)


@functools.lru_cache(maxsize=None)
def _channel_dft():
    n = B_GROUP_DIM
    k = (np.arange(n)[:, None] * np.arange(n)[None, :]) % n
    ang = 2.0 * np.pi * k / n
    m = np.concatenate([np.cos(ang), np.sin(ang)], axis=1) / np.sqrt(n)
    return m.astype(np.float32)


@functools.lru_cache(maxsize=None)
def _stage1_dft(n1):
    k = (np.arange(n1)[:, None] * np.arange(n1)[None, :]) % n1
    ang = 2.0 * np.pi * k / n1
    c, s = np.cos(ang), np.sin(ang)
    m = np.block([[c, -s], [-s, -c]]) / np.sqrt(n1)
    return m.astype(np.float32)


@functools.lru_cache(maxsize=None)
def _stage2_dft(n1, n2):
    s = n1 * n2
    k1 = np.arange(n1)[:, None, None]
    k2 = np.arange(n2)[None, :, None]
    s2 = np.arange(n2)[None, None, :]
    ang = 2.0 * np.pi * ((s2 * (k1 + n1 * k2)) % s) / s
    m = np.concatenate([np.cos(ang), np.sin(ang)], axis=2) / np.sqrt(n2)
    return m.astype(np.float32)


def _mixer_ab_front_kernel(x_ref, g_ref, win_ref, gain_ref, ws_ref, bs_ref,
                           a_ref, f_ref, *, tm):
    h = _rmsnorm(x_ref[...], g_ref[...]).astype(BF16)
    proj = _dot(h, win_ref[...])
    for hd in range(A_HEADS):
        lanes = slice(hd * A_HEAD_DIM, (hd + 1) * A_HEAD_DIM)
        u = _gelu_tanh(proj[:, hd * A_HEAD_DIM:(hd + 1) * A_HEAD_DIM])
        v = _gelu_tanh(proj[:, D_A + hd * A_HEAD_DIM:D_A + (hd + 1) * A_HEAD_DIM])
        vn = _rmsnorm(v, gain_ref[:, lanes]).astype(BF16)
        chunks = [slice(n * CHUNK, (n + 1) * CHUNK) for n in range(tm // CHUNK)]
        s_all = _dot(ws_ref[hd], jnp.concatenate([vn[rows, :] for rows in chunks], axis=1))
        for n, rows in enumerate(chunks):
            s = s_all[:, n * A_HEAD_DIM:(n + 1) * A_HEAD_DIM] + bs_ref[:, lanes]
            a_ref[rows, lanes] = (u[rows, :] * s).astype(BF16)
    f_ref[...] = proj[:, 2 * D_A:].astype(BF16)


def _mixer_ab_front(x, g, w_in, gain, w_s, bs_tile, *, tm):
    t = x.shape[0]
    row_f32 = pl.BlockSpec((tm, D_MODEL), lambda i: (i, 0))
    row_half = pl.BlockSpec((tm, D_A), lambda i: (i, 0))
    out = jax.ShapeDtypeStruct((t, D_A), BF16)
    return pl.pallas_call(
        functools.partial(_mixer_ab_front_kernel, tm=tm),
        out_shape=(out, out),
        grid=(t // tm,),
        in_specs=[row_f32, _resident(g.shape), _resident(w_in.shape), _resident(gain.shape),
                  _resident(w_s.shape), _resident(bs_tile.shape)],
        out_specs=(row_half, row_half),
        compiler_params=_params(),
        name="mixer_ab_front",
    )(x, g, w_in, gain, w_s, bs_tile)


def _fourier_kernel(f_ref, cs_ref, m1_ref, g2_ref, o_ref, a_scr, tr_scr, ti_scr,
                    *, n1, n2, groups, jb):
    pf = n2 + SLAB_PAD
    pt = n1 + SLAB_PAD
    for grp in range(groups):
        lanes = slice(grp * B_GROUP_DIM, (grp + 1) * B_GROUP_DIM)

        def copy_in(s1, carry):
            rows = pl.ds(pl.multiple_of(s1 * n2, n2), n2)
            a_scr[pl.ds(s1 * pf, n2), :] = f_ref[rows, lanes].astype(F32)
            return carry

        def stage1(it, carry):
            j0 = it * jb
            x = jnp.concatenate([a_scr[pl.ds(j0 + t, n1, stride=pf), :] for t in range(jb)], axis=0)
            pq = _dot(x.astype(BF16), cs_ref[...]).astype(BF16)
            p = jnp.concatenate([pq[t * n1:(t + 1) * n1, :B_GROUP_DIM] for t in range(jb)], axis=1)
            q = jnp.concatenate([pq[t * n1:(t + 1) * n1, B_GROUP_DIM:] for t in range(jb)], axis=1)
            t_all = _dot(m1_ref[...], jnp.concatenate([p, q], axis=0))
            for t in range(jb):
                cols = slice(t * B_GROUP_DIM, (t + 1) * B_GROUP_DIM)
                tr_scr[pl.ds((j0 + t) * pt, n1), :] = t_all[:n1, cols]
                ti_scr[pl.ds((j0 + t) * pt, n1), :] = t_all[n1:, cols]
            return carry

        def stage2(k1, carry):
            tr = tr_scr[pl.ds(k1, n2, stride=pt), :]
            ti = ti_scr[pl.ds(k1, n2, stride=pt), :]
            rhs = jnp.concatenate([tr, ti], axis=0).astype(BF16)
            a_scr[pl.ds(k1, n2, stride=pt), :] = _dot(g2_ref[k1], rhs)
            return carry

        def copy_out(k2, carry):
            rows = pl.ds(pl.multiple_of(k2 * n1, n1), n1)
            o_ref[rows, lanes] = a_scr[pl.ds(k2 * pt, n1), :].astype(BF16)
            return carry

        lax.fori_loop(0, n1, copy_in, 0, unroll=min(n1, 8))
        lax.fori_loop(0, n2 // jb, stage1, 0, unroll=min(n2 // jb, DFT_UNROLL1))
        lax.fori_loop(0, n1, stage2, 0, unroll=min(n1, DFT_UNROLL2))
        lax.fori_loop(0, n2, copy_out, 0, unroll=8)


def _fourier_mix(f, cs, batch, seq):
    n2 = DFT_N2
    n1 = seq // n2
    jb = V7X_MXU_DIM // n1
    assert n1 * n2 == seq and n1 % BF16_SUBLANES == 0 and V7X_MXU_DIM % n1 == 0 and n2 % jb == 0
    groups = B_GROUPS if seq * D_B * 2 <= DFT_BLOCK_BYTES else 1
    m1 = jnp.asarray(_stage1_dft(n1)).astype(BF16)
    g2 = jnp.asarray(_stage2_dft(n1, n2)).astype(BF16)
    blk = pl.BlockSpec((seq, groups * B_GROUP_DIM), lambda b, g: (b, g))
    slab_rows = max(n1 * (n2 + SLAB_PAD), n2 * (n1 + SLAB_PAD))
    return pl.pallas_call(
        functools.partial(_fourier_kernel, n1=n1, n2=n2, groups=groups, jb=jb),
        out_shape=jax.ShapeDtypeStruct((batch * seq, D_B), BF16),
        grid=(batch, B_GROUPS // groups),
        in_specs=[blk, _resident(cs.shape), _resident(m1.shape), _resident(g2.shape)],
        out_specs=blk,
        scratch_shapes=[pltpu.VMEM((slab_rows, B_GROUP_DIM), F32),
                        pltpu.VMEM((n2 * (n1 + SLAB_PAD), B_GROUP_DIM), F32),
                        pltpu.VMEM((n2 * (n1 + SLAB_PAD), B_GROUP_DIM), F32)],
        compiler_params=_params(2),
        name="fourier_mix",
    )(f, cs, m1, g2)


def _halo_specs(tm, n_tiles, halo, width):
    per = tm // halo
    last = n_tiles * per - 1
    prev = pl.BlockSpec((halo, width), lambda i: (jnp.maximum(i * per - 1, 0), 0))
    main = pl.BlockSpec((tm, width), lambda i: (i, 0))
    nxt = pl.BlockSpec((halo, width), lambda i: (jnp.minimum((i + 1) * per, last), 0))
    return prev, main, nxt


def _store_normed_with_halo(h_scr, before, tile, after, g, *, seq_tiles):
    pos = lax.rem(pl.program_id(0), seq_tiles)
    hp = jnp.where(pos != 0, _rmsnorm(before, g), 0.0)
    hn = jnp.where(pos != seq_tiles - 1, _rmsnorm(after, g), 0.0)
    h_scr[...] = jnp.concatenate([hp, _rmsnorm(tile, g), hn], axis=0).astype(BF16)


def _conv_scratch(rows):
    return pltpu.VMEM((CONV_SLOTS, FF_COLS // LANES, rows, LANES), F32)


def _store_slabs(scr, slot, row0, value):
    for s in range(FF_COLS // LANES):
        scr[slot, s, row0:row0 + value.shape[0], :] = value[:, s * LANES:(s + 1) * LANES]


def _dwconv3_rows(scr, slot, w, b, row0, n):
    out = []
    for s in range(FF_COLS // LANES):
        ws = w[:, s * LANES:(s + 1) * LANES]
        out.append(scr[slot, s, pl.ds(HALO + row0 - 1, n), :] * ws[0:1, :]
                   + scr[slot, s, pl.ds(HALO + row0, n), :] * ws[1:2, :]
                   + scr[slot, s, pl.ds(HALO + row0 + 1, n), :] * ws[2:3, :]
                   + b[:, s * LANES:(s + 1) * LANES])
    return jnp.concatenate(out, axis=-1)


def _row_chunks(n, parts, align):
    cuts = [min(n, -(-(n * p // parts) // align) * align) for p in range(parts + 1)]
    return [(lo, hi) for lo, hi in zip(cuts[:-1], cuts[1:]) if hi > lo]


def _gate_cols(c, n=1):
    return slice(c * FF_COLS, (c + n) * FF_COLS)


def _val_cols(c):
    return slice(D_FF + c * FF_COLS, D_FF + (c + 1) * FF_COLS)


def _ffn_pipeline(h_scr, pg_scr, pv_scr, wup_ref, cw_ref, cb_ref, wdn_ref, o_ref, *, tm,
                  finish_rows=None):
    n_blocks = D_FF // FF_COLS

    def up_project(c, parts):
        slot = c % CONV_SLOTS
        for lo, hi in _row_chunks(tm + 2 * HALO, parts, BF16_SUBLANES):
            h = h_scr[lo:hi, :]
            _store_slabs(pg_scr, slot, lo, _dot(h, wup_ref[:, _gate_cols(c)]))
            _store_slabs(pv_scr, slot, lo, _dot(h, wup_ref[:, _val_cols(c)]))

    def conv_act(c, lo, hi):
        slot = c % CONV_SLOTS
        gate = _dwconv3_rows(pg_scr, slot, cw_ref[:, _gate_cols(c)], cb_ref[:, _gate_cols(c)],
                             lo, hi - lo)
        val = _dwconv3_rows(pv_scr, slot, cw_ref[:, _val_cols(c)], cb_ref[:, _val_cols(c)],
                            lo, hi - lo)
        return (_silu(gate) * val).astype(BF16)

    for c in range(min(UP_AHEAD, n_blocks)):
        up_project(c, EDGE_SPLIT if c == 0 else 1)
    for c0 in range(0, n_blocks, DOWN_GROUP):
        group = range(c0, min(c0 + DOWN_GROUP, n_blocks))
        last = group[-1] == n_blocks - 1
        for lo, hi in _row_chunks(tm, EDGE_SPLIT if last else 1, BF16_SUBLANES):
            acts = []
            for c in group:
                if lo == 0 and c + UP_AHEAD < n_blocks:
                    up_project(c + UP_AHEAD, 1)
                acts.append(conv_act(c, lo, hi))
            o_ref[lo:hi, :] += _dot(jnp.concatenate(acts, axis=1),
                                    wdn_ref[_gate_cols(c0, len(group)), :])
            if finish_rows is not None and last:
                finish_rows(lo, hi)


def _conv_ffn_kernel(xp_ref, x_ref, xn_ref, g_ref, wup_ref, cw_ref, cb_ref, wdn_ref, fin_ref,
                     o_ref, h_scr, pg_scr, pv_scr, *, tm, seq_tiles, final):
    _store_normed_with_halo(h_scr, xp_ref[...], x_ref[...], xn_ref[...], g_ref[...],
                            seq_tiles=seq_tiles)
    o_ref[...] = x_ref[...]

    def final_norm(lo, hi):
        o_ref[lo:hi, :] = _rmsnorm(o_ref[lo:hi, :], fin_ref[...])

    _ffn_pipeline(h_scr, pg_scr, pv_scr, wup_ref, cw_ref, cb_ref, wdn_ref, o_ref, tm=tm,
                  finish_rows=final_norm if final else None)


def _mix_out_conv_ffn_kernel(xp_ref, x_ref, xn_ref, ap_ref, a_ref, an_ref, fp_ref, f_ref, fn_ref,
                             wmix_ref, g_ref, wup_ref, cw_ref, cb_ref, wdn_ref,
                             o_ref, h_scr, pg_scr, pv_scr, *, tm, seq_tiles):
    mix = jnp.concatenate(
        [jnp.concatenate([ap_ref[...], a_ref[...], an_ref[...]], axis=0),
         jnp.concatenate([fp_ref[...], f_ref[...], fn_ref[...]], axis=0)], axis=1)
    x_cat = jnp.concatenate([xp_ref[...], x_ref[...], xn_ref[...]], axis=0)
    x_all = jnp.concatenate(
        [x_cat[lo:hi, :] + _dot(mix[lo:hi, :], wmix_ref[...])
         for lo, hi in _row_chunks(tm + 2 * MIX_HALO, EDGE_SPLIT, BF16_SUBLANES)], axis=0)
    lo = MIX_HALO - HALO
    _store_normed_with_halo(h_scr, x_all[lo:MIX_HALO], x_all[MIX_HALO:MIX_HALO + tm],
                            x_all[MIX_HALO + tm:MIX_HALO + tm + HALO], g_ref[...],
                            seq_tiles=seq_tiles)
    o_ref[...] = x_all[MIX_HALO:MIX_HALO + tm]
    _ffn_pipeline(h_scr, pg_scr, pv_scr, wup_ref, cw_ref, cb_ref, wdn_ref, o_ref, tm=tm)


def _ffn_scratch(tm):
    rows = tm + 2 * HALO
    return [pltpu.VMEM((rows, D_MODEL), BF16), _conv_scratch(rows), _conv_scratch(rows)]


def _conv_ffn(x, g, w_up, conv_w, conv_b, w_down, fin, *, layer, tm, seq, final):
    t = x.shape[0]
    n_tiles = t // tm
    prev, main, nxt = _halo_specs(tm, n_tiles, HALO, D_MODEL)
    return pl.pallas_call(
        functools.partial(_conv_ffn_kernel, tm=tm, seq_tiles=seq // tm, final=final),
        out_shape=jax.ShapeDtypeStruct((t, D_MODEL), F32),
        grid=(n_tiles,),
        in_specs=[prev, main, nxt, _resident(g.shape), _resident_layer(w_up.shape, layer),
                  _resident(conv_w.shape), _resident(conv_b.shape),
                  _resident_layer(w_down.shape, layer), _resident(fin.shape)],
        out_specs=main,
        scratch_shapes=_ffn_scratch(tm),
        compiler_params=_params(),
        name="conv_ffn",
    )(x, x, x, g, w_up, conv_w, conv_b, w_down, fin)


def _mix_out_conv_ffn(x, a, ff, w_mix, g, w_up, conv_w, conv_b, w_down, *, layer, tm, seq):
    t = x.shape[0]
    n_tiles = t // tm
    x_specs = _halo_specs(tm, n_tiles, MIX_HALO, D_MODEL)
    half_specs = _halo_specs(tm, n_tiles, MIX_HALO, D_A)
    return pl.pallas_call(
        functools.partial(_mix_out_conv_ffn_kernel, tm=tm, seq_tiles=seq // tm),
        out_shape=jax.ShapeDtypeStruct((t, D_MODEL), F32),
        grid=(n_tiles,),
        in_specs=[*x_specs, *half_specs, *half_specs, _resident(w_mix.shape), _resident(g.shape),
                  _resident_layer(w_up.shape, layer), _resident(conv_w.shape),
                  _resident(conv_b.shape), _resident_layer(w_down.shape, layer)],
        out_specs=x_specs[1],
        scratch_shapes=_ffn_scratch(tm),
        compiler_params=_params(),
        name="mix_out_conv_ffn",
    )(x, x, x, a, a, a, ff, ff, ff, w_mix, g, w_up, conv_w, conv_b, w_down)


def _mixer_c_kernel(xp_ref, x_ref, xn_ref, g_ref, win_ref, cw_ref, cb_ref, wout_ref,
                    o_ref, h_scr, cz_scr, y_scr, *, tm, seq_tiles):
    _store_normed_with_halo(h_scr, xp_ref[...], x_ref[...], xn_ref[...], g_ref[...],
                            seq_tiles=seq_tiles)
    h = h_scr[...]
    for c in range(D_C // FF_COLS):
        slot = c % CONV_SLOTS
        cols = slice(c * FF_COLS, (c + 1) * FF_COLS)
        gate_b = _dot(h, _proj_cols(win_ref, 0, c))[HALO:HALO + tm, :]
        gate_c = _dot(h, _proj_cols(win_ref, 1, c))
        z = _dot(h, _proj_cols(win_ref, 2, c))
        _store_slabs(cz_scr, slot, 0, gate_c * z)
        conv = _dwconv3_rows(cz_scr, slot, cw_ref[:, cols], cb_ref[:, cols], 0, tm)
        y_scr[:, cols] = (gate_b * conv).astype(BF16)
    o_ref[...] = x_ref[...] + _dot(y_scr[...], wout_ref[...])


def _proj_cols(w_ref, part, c):
    start = part * D_C + c * FF_COLS
    return w_ref[:, start:start + FF_COLS]


def _mixer_c(x, g, w_in, conv_w, conv_b, w_out, *, tm, seq):
    t = x.shape[0]
    n_tiles = t // tm
    prev, main, nxt = _halo_specs(tm, n_tiles, HALO, D_MODEL)
    rows = tm + 2 * HALO
    return pl.pallas_call(
        functools.partial(_mixer_c_kernel, tm=tm, seq_tiles=seq // tm),
        out_shape=jax.ShapeDtypeStruct((t, D_MODEL), F32),
        grid=(n_tiles,),
        in_specs=[prev, main, nxt, _resident(g.shape), _resident(w_in.shape),
                  _resident(conv_w.shape), _resident(conv_b.shape), _resident(w_out.shape)],
        out_specs=main,
        scratch_shapes=[pltpu.VMEM((rows, D_MODEL), BF16),
                        _conv_scratch(rows),
                        pltpu.VMEM((tm, D_C), BF16)],
        compiler_params=_params(),
        name="mixer_c",
    )(x, x, x, g, w_in, conv_w, conv_b, w_out)


def _trunk(x3, wts):
    batch, seq, _ = x3.shape
    tm = min(TOKEN_TILE, seq)
    tm_mix = min(MIXER_TILE, seq)
    assert seq % tm == 0 and seq % tm_mix == 0 and tm_mix % CHUNK == 0 and seq % DFT_N2 == 0
    x = x3.reshape(batch * seq, D_MODEL)
    depth = wts["norm_mix"].shape[0]
    assert depth % 2 == 0
    for l in range(depth):
        i = l // 2
        g_mix = wts["norm_mix"][l][None, :]
        ffn = (wts["norm_ffn"][l][None, :], wts["w_up"], wts["ffn_conv_w"][l],
               wts["ffn_conv_b"][l], wts["w_down"])
        if l % 2 == 0:
            a, f = _mixer_ab_front(x, g_mix, wts["w_in_ab"][i], wts["sgu_gain"][i],
                                   wts["w_s"][i], wts["bs_tile"][i], tm=tm_mix)
            ff = _fourier_mix(f, wts["cs"], batch, seq)
            x = _mix_out_conv_ffn(x, a, ff, wts["w_out_ab"][i], *ffn, layer=l, tm=tm, seq=seq)
        else:
            x = _mixer_c(x, g_mix, wts["w_in_c"][i], wts["conv_w_c"][i], wts["conv_b_c"][i],
                         wts["w_out_c"][i], tm=tm_mix, seq=seq)
            x = _conv_ffn(x, *ffn, wts["final_norm"], layer=l, tm=tm, seq=seq,
                          final=(l == depth - 1))
    return x.reshape(batch, seq, D_MODEL)


def kernel(x_prompt, x_sample, norm_mix, w_in_ab, sgu_gain, w_s, b_s, w_out_ab, w_in_c, conv_w_c,
           conv_b_c, w_out_c, norm_ffn, w_up, ffn_conv_w, ffn_conv_b, w_down, final_norm):
    n_even = w_in_ab.shape[0]
    wts = dict(
        norm_mix=norm_mix, norm_ffn=norm_ffn, final_norm=final_norm[None, :],
        w_in_ab=w_in_ab.astype(BF16), w_out_ab=w_out_ab.astype(BF16),
        w_s=w_s.astype(BF16), sgu_gain=sgu_gain.reshape(n_even, 1, D_A),
        bs_tile=jnp.repeat(jnp.swapaxes(b_s, 1, 2), A_HEAD_DIM, axis=2),
        cs=jnp.asarray(_channel_dft()).astype(BF16),
        w_in_c=w_in_c.astype(BF16), w_out_c=w_out_c.astype(BF16),
        conv_w_c=conv_w_c, conv_b_c=conv_b_c[:, None, :],
        w_up=w_up.astype(BF16), w_down=w_down.astype(BF16),
        ffn_conv_w=ffn_conv_w, ffn_conv_b=ffn_conv_b[:, None, :],
    )
    return _trunk(x_prompt, wts), _trunk(x_sample, wts)
```
